```python
import jax, jax.numpy as jnp
from jax import lax
import numpy as np

D_MODEL = 1024
BATCH = 16
SEQ = 4096
DEPTH = 1

PLE_DIM = 256
CHUNK = 128
A_WIDTH = 1024
A_GROUPS = 8
A_GROUP_DIM = A_WIDTH // A_GROUPS
B_WIDTH = 1024
B_CONV = 31
FFN_DIM = 2816
FFN_CONV = 3
IN_COLS = 2 * A_WIDTH + 2 * B_WIDTH + 2 * D_MODEL
EPS_RMS = 1e-6
EPS_LN = 1e-5

kernel_name = "hybrid_gmlp_conformer_gated_block"


def rmsnorm(x, g):
    xf = x.astype(jnp.float32)
    y = xf * lax.rsqrt(jnp.mean(xf * xf, axis=-1, keepdims=True) + EPS_RMS)
    return y.astype(x.dtype) * g


def layernorm(x, g, b):
    xf = x.astype(jnp.float32)
    mu = jnp.mean(xf, axis=-1, keepdims=True)
    var = jnp.mean(jnp.square(xf - mu), axis=-1, keepdims=True)
    y = (xf - mu) * lax.rsqrt(var + EPS_LN)
    return y.astype(x.dtype) * g + b


def causal_dwconv(x, w, b):
    k, c = w.shape
    y = lax.conv_general_dilated(
        x, w[:, None, :], window_strides=(1,), padding=((k - 1, 0),),
        dimension_numbers=("NWC", "WIO", "NWC"), feature_group_count=c)
    return y + b


def setup_inputs(seed: int = 0) -> dict:
    key = jax.random.key(seed)
    ks = jax.random.split(key, 32)
    f32 = jnp.float32

    def nrm(k, shape, scale):
        return jax.random.normal(k, shape, f32) * scale

    def gain(k, shape):
        return 1.0 + 0.01 * jax.random.normal(k, shape, f32)

    L = DEPTH
    return {
        "x": jax.random.normal(ks[0], (BATCH, SEQ, D_MODEL), f32),
        "p": jax.random.normal(ks[1], (DEPTH, BATCH, SEQ, PLE_DIM), f32),
        "g_mix": gain(ks[2], (L, D_MODEL)),
        "w_in": nrm(ks[3], (L, D_MODEL, IN_COLS), D_MODEL ** -0.5),
        "ln_v_g": gain(ks[4], (L, A_WIDTH)),
        "ln_v_b": nrm(ks[5], (L, A_WIDTH), 0.01),
        "w_s": nrm(ks[6], (L, A_GROUPS, CHUNK, CHUNK), CHUNK ** -0.5),
        "b_s": 1.0 + 0.1 * jax.random.normal(ks[7], (L, A_GROUPS, CHUNK), f32),
        "w_a_out": nrm(ks[8], (L, A_WIDTH, D_MODEL), A_WIDTH ** -0.5),
        "conv_b_w": nrm(ks[9], (L, B_CONV, B_WIDTH), B_CONV ** -0.5),
        "conv_b_b": nrm(ks[10], (L, B_WIDTH), 0.01),
        "ln_b_g": gain(ks[11], (L, B_WIDTH)),
        "ln_b_b": nrm(ks[12], (L, B_WIDTH), 0.01),
        "w_b_out": nrm(ks[13], (L, B_WIDTH, D_MODEL), B_WIDTH ** -0.5),
        "w_o": nrm(ks[14], (L, D_MODEL, D_MODEL), D_MODEL ** -0.5),
        "g_ffn": gain(ks[15], (L, D_MODEL)),
        "w_up": nrm(ks[16], (L, D_MODEL, 2 * FFN_DIM), D_MODEL ** -0.5),
        "ffn_conv_w": nrm(ks[17], (L, FFN_CONV, 2 * FFN_DIM), FFN_CONV ** -0.5),
        "ffn_conv_b": nrm(ks[18], (L, 2 * FFN_DIM), 0.01),
        "w_down": nrm(ks[19], (L, FFN_DIM, D_MODEL), FFN_DIM ** -0.5),
        "g_pg": gain(ks[20], (L, D_MODEL)),
        "w_pg": nrm(ks[21], (L, D_MODEL, D_MODEL), D_MODEL ** -0.5),
        "w_ple": nrm(ks[22], (L, PLE_DIM, D_MODEL), PLE_DIM ** -0.5),
        "g_ple": gain(ks[23], (L, D_MODEL)),
        "g_final": gain(ks[24], (D_MODEL,)),
    }


def reference(x, p, g_mix, w_in, ln_v_g, ln_v_b, w_s, b_s, w_a_out,
              conv_b_w, conv_b_b, ln_b_g, ln_b_b, w_b_out, w_o,
              g_ffn, w_up, ffn_conv_w, ffn_conv_b, w_down,
              g_pg, w_pg, w_ple, g_ple, g_final):
    bsz, seq = x.shape[0], x.shape[1]
    n_chunks = seq // CHUNK
    tril = jnp.tril(jnp.ones((CHUNK, CHUNK), dtype=bool))
    splits = [A_WIDTH, 2 * A_WIDTH, 2 * A_WIDTH + B_WIDTH,
              2 * A_WIDTH + 2 * B_WIDTH, 2 * A_WIDTH + 2 * B_WIDTH + D_MODEL]

    for i in range(DEPTH):
        h = rmsnorm(x, g_mix[i])
        z = h @ w_in[i]
        u, v, a_b, gl_b, gate_a, gate_b = jnp.split(z, splits, axis=-1)

        u = jax.nn.gelu(u)
        v = layernorm(jax.nn.gelu(v), ln_v_g[i], ln_v_b[i])
        vr = v.reshape(bsz, n_chunks, CHUNK, A_GROUPS, A_GROUP_DIM)
        ws = jnp.where(tril[None], w_s[i], jnp.zeros_like(w_s[i]))
        mix = jnp.einsum("gts,bnsgc->bntgc", ws, vr) + b_s[i].T[:, :, None]
        y_a = (u * mix.reshape(bsz, seq, A_WIDTH)) @ w_a_out[i]

        glu = a_b * jax.nn.sigmoid(gl_b)
        c = causal_dwconv(glu, conv_b_w[i], conv_b_b[i])
        c = jax.nn.silu(layernorm(c, ln_b_g[i], ln_b_b[i]))
        y_b = c @ w_b_out[i]

        merged = jax.nn.sigmoid(gate_a) * y_a + jax.nn.sigmoid(gate_b) * y_b
        x = x + merged @ w_o[i]

        h = rmsnorm(x, g_ffn[i])
        up = causal_dwconv(h @ w_up[i], ffn_conv_w[i], ffn_conv_b[i])
        gate, val = jnp.split(up, 2, axis=-1)
        x = x + (jax.nn.gelu(gate) * val) @ w_down[i]

        pe = rmsnorm(p[i] @ w_ple[i], g_ple[i])
        pg = jax.nn.sigmoid(rmsnorm(x, g_pg[i]) @ w_pg[i])
        x = x + pe * pg

    return rmsnorm(x, g_final)
```

```python
import functools
import math

import jax
import jax.numpy as jnp
from jax.experimental import pallas as pl
from jax.experimental.pallas import tpu as pltpu

F32 = jnp.float32
BF16 = jnp.bfloat16

LANES = 128
SUBLANES = 8
MXU_COLS = 256
ROW_BLOCK = 256
CHUNK = 128
EPS_RMS = 1e-6
EPS_LN = 1e-5
VMEM_LIMIT_BYTES = 56 * 1024 * 1024

_GELU_C = math.sqrt(2.0 / math.pi)


def _gelu(x):
    inner = x * (_GELU_C + (_GELU_C * 0.044715) * (x * x))
    return (0.5 * x) * (1.0 + jnp.tanh(inner))


def _sigmoid(x):
    return 0.5 + 0.5 * jnp.tanh(0.5 * x)


def _rmsnorm(x, g):
    ms = jnp.mean(x * x, axis=-1, keepdims=True)
    return (x * jax.lax.rsqrt(ms + EPS_RMS)) * g


def _layernorm(x, g, b):
    mu = jnp.mean(x, axis=-1, keepdims=True)
    xc = x - mu
    var = jnp.mean(xc * xc, axis=-1, keepdims=True)
    return (xc * jax.lax.rsqrt(var + EPS_LN)) * g + b


def _dot(a, b):
    return jnp.dot(a, b, preferred_element_type=F32)


def _mixer_kernel(x_ref, g_mix_ref, w_in_ref, ln_v_g_ref, ln_v_b_ref, ws_ref, bs_ref,
                  w_a_ref, cw_ref, cb_ref, ln_b_g_ref, ln_b_b_ref, w_b_ref, w_o_ref,
                  o_ref, cbuf, *, blocks_per_seq, halo, taps, width):
    rb = x_ref.shape[0]
    n_lane_blocks = width // LANES
    n_chunks = rb // CHUNK

    @pl.when(pl.program_id(0) % blocks_per_seq == 0)
    def _():
        cbuf[:, 0:halo, :] = jnp.zeros((n_lane_blocks, halo, LANES), F32)

    x = x_ref[...]
    h = _rmsnorm(x, g_mix_ref[...]).astype(BF16)

    def zcols(k):
        return _dot(h, w_in_ref[:, k * width:(k + 1) * width])

    u = _gelu(zcols(0))
    v = _layernorm(_gelu(zcols(1)), ln_v_g_ref[...], ln_v_b_ref[...]).astype(BF16)
    ua_cols = []
    for g in range(n_lane_blocks):
        cols = slice(g * LANES, (g + 1) * LANES)
        rhs = jnp.concatenate([v[n * CHUNK:(n + 1) * CHUNK, cols] for n in range(n_chunks)], axis=1)
        mix = _dot(ws_ref[g], rhs)
        mix = jnp.concatenate([mix[:, n * LANES:(n + 1) * LANES] for n in range(n_chunks)], axis=0)
        bias = jnp.concatenate([bs_ref[g]] * n_chunks, axis=0)
        ua_cols.append(u[:, cols] * (mix + bias))
    ua = jnp.concatenate(ua_cols, axis=1).astype(BF16)
    y_a = _dot(ua, w_a_ref[...])

    glu = zcols(2) * _sigmoid(zcols(3))
    for j in range(n_lane_blocks):
        cbuf[j, halo:halo + rb, :] = glu[:, j * LANES:(j + 1) * LANES]
    conv_cols = []
    for j in range(n_lane_blocks):
        cols = slice(j * LANES, (j + 1) * LANES)
        acc = jnp.zeros((rb, LANES), F32) + cb_ref[:, cols]
        for k in range(taps):
            lo = halo - (taps - 1) + k
            acc = acc + cbuf[j, lo:lo + rb, :] * cw_ref[k:k + 1, cols]
        conv_cols.append(acc)
    for j in range(n_lane_blocks):
        cbuf[j, 0:halo, :] = cbuf[j, rb:rb + halo, :]
    c = _layernorm(jnp.concatenate(conv_cols, axis=1), ln_b_g_ref[...], ln_b_b_ref[...])
    c = (c * _sigmoid(c)).astype(BF16)
    y_b = _dot(c, w_b_ref[...])

    merged = _sigmoid(zcols(4)) * y_a + _sigmoid(zcols(5)) * y_b
    o_ref[...] = x + _dot(merged.astype(BF16), w_o_ref[...])


def _ffn_kernel(x_ref, p_ref, g_ffn_ref, w_up_ref, fw_ref, fb_ref, w_down_ref,
                g_pg_ref, w_pg_ref, w_ple_ref, g_ple_ref, g_final_ref,
                o_ref, ubuf, act, *, blocks_per_seq, halo, taps, ffn_dim, final_norm):
    rb = x_ref.shape[0]
    n_up_blocks = (2 * ffn_dim) // LANES
    n_gate_blocks = ffn_dim // LANES

    @pl.when(pl.program_id(0) % blocks_per_seq == 0)
    def _():
        ubuf[:, 0:halo, :] = jnp.zeros((n_up_blocks, halo, LANES), F32)

    x = x_ref[...]
    h = _rmsnorm(x, g_ffn_ref[...]).astype(BF16)
    for jb in range((2 * ffn_dim) // MXU_COLS):
        r = _dot(h, w_up_ref[:, jb * MXU_COLS:(jb + 1) * MXU_COLS])
        for q in range(MXU_COLS // LANES):
            ubuf[jb * (MXU_COLS // LANES) + q, halo:halo + rb, :] = r[:, q * LANES:(q + 1) * LANES]

    def conv(j):
        cols = slice(j * LANES, (j + 1) * LANES)
        acc = jnp.zeros((rb, LANES), F32) + fb_ref[:, cols]
        for k in range(taps):
            lo = halo - (taps - 1) + k
            acc = acc + ubuf[j, lo:lo + rb, :] * fw_ref[k:k + 1, cols]
        return acc

    for j in range(n_gate_blocks):
        a = _gelu(conv(j)) * conv(j + n_gate_blocks)
        act[:, j * LANES:(j + 1) * LANES] = a.astype(BF16)
    for j in range(n_up_blocks):
        ubuf[j, 0:halo, :] = ubuf[j, rb:rb + halo, :]
    x = x + _dot(act[...], w_down_ref[...])

    pe = _rmsnorm(_dot(p_ref[...].astype(BF16), w_ple_ref[...]), g_ple_ref[...])
    pg = _sigmoid(_dot(_rmsnorm(x, g_pg_ref[...]).astype(BF16), w_pg_ref[...]))
    x = x + pe * pg
    if final_norm:
        x = _rmsnorm(x, g_final_ref[...])
    o_ref[...] = x


def _resident(shape):
    zeros = (0,) * len(shape)
    return pl.BlockSpec(shape, lambda i: zeros, pipeline_mode=pl.Buffered(1))


def _row(v):
    return v.reshape(1, -1).astype(F32)


def _mixer_call(x2d, seq, g_mix, w_in, ln_v_g, ln_v_b, w_s, b_s, w_a_out,
                conv_w, conv_b, ln_b_g, ln_b_b, w_b_out, w_o):
    t, d = x2d.shape
    width = w_a_out.shape[0]
    taps = conv_w.shape[0]
    groups, chunk, _ = w_s.shape
    assert chunk == CHUNK and groups * LANES == width and w_b_out.shape[0] == width
    assert w_in.shape[1] == 6 * width and d == width
    assert seq % ROW_BLOCK == 0 and ROW_BLOCK % CHUNK == 0
    halo = -(-(taps - 1) // SUBLANES) * SUBLANES
    tril = jnp.tril(jnp.ones((chunk, chunk), dtype=bool))
    ws = jnp.where(tril[None], w_s, jnp.zeros_like(w_s)).astype(BF16)
    bs = jnp.broadcast_to(b_s[:, :, None], (groups, chunk, LANES)).astype(F32)
    operands = [
        x2d, _row(g_mix), w_in.astype(BF16), _row(ln_v_g), _row(ln_v_b), ws, bs,
        w_a_out.astype(BF16), conv_w.astype(F32), _row(conv_b), _row(ln_b_g), _row(ln_b_b),
        w_b_out.astype(BF16), w_o.astype(BF16),
    ]
    row_spec = pl.BlockSpec((ROW_BLOCK, d), lambda i: (i, 0))
    in_specs = [row_spec] + [_resident(a.shape) for a in operands[1:]]
    body = functools.partial(_mixer_kernel, blocks_per_seq=seq // ROW_BLOCK, halo=halo,
                             taps=taps, width=width)
    return pl.pallas_call(
        body,
        grid=(t // ROW_BLOCK,),
        in_specs=in_specs,
        out_specs=row_spec,
        out_shape=jax.ShapeDtypeStruct((t, d), F32),
        scratch_shapes=[pltpu.VMEM((width // LANES, halo + ROW_BLOCK, LANES), F32)],
        compiler_params=pltpu.CompilerParams(
            dimension_semantics=("arbitrary",), vmem_limit_bytes=VMEM_LIMIT_BYTES),
        name="token_mixer",
    )(*operands)


def _ffn_call(x2d, p2d, seq, g_ffn, w_up, ffn_conv_w, ffn_conv_b, w_down,
              g_pg, w_pg, w_ple, g_ple, g_final, final_norm):
    t, d = x2d.shape
    ffn_dim = w_down.shape[0]
    taps = ffn_conv_w.shape[0]
    assert w_up.shape[1] == 2 * ffn_dim and ffn_dim % MXU_COLS == 0
    halo = -(-(taps - 1) // SUBLANES) * SUBLANES
    operands = [
        x2d, p2d, _row(g_ffn), w_up.astype(BF16), ffn_conv_w.astype(F32), _row(ffn_conv_b),
        w_down.astype(BF16), _row(g_pg), w_pg.astype(BF16), w_ple.astype(BF16), _row(g_ple),
        _row(g_final),
    ]
    row_spec = pl.BlockSpec((ROW_BLOCK, d), lambda i: (i, 0))
    p_spec = pl.BlockSpec((ROW_BLOCK, p2d.shape[1]), lambda i: (i, 0))
    in_specs = [row_spec, p_spec] + [_resident(a.shape) for a in operands[2:]]
    body = functools.partial(_ffn_kernel, blocks_per_seq=seq // ROW_BLOCK, halo=halo,
                             taps=taps, ffn_dim=ffn_dim, final_norm=final_norm)
    return pl.pallas_call(
        body,
        grid=(t // ROW_BLOCK,),
        in_specs=in_specs,
        out_specs=row_spec,
        out_shape=jax.ShapeDtypeStruct((t, d), F32),
        scratch_shapes=[
            pltpu.VMEM(((2 * ffn_dim) // LANES, halo + ROW_BLOCK, LANES), F32),
            pltpu.VMEM((ROW_BLOCK, ffn_dim), BF16),
        ],
        compiler_params=pltpu.CompilerParams(
            dimension_semantics=("arbitrary",), vmem_limit_bytes=VMEM_LIMIT_BYTES),
        name="channel_mixer",
    )(*operands)


def kernel(x, p, g_mix, w_in, ln_v_g, ln_v_b, w_s, b_s, w_a_out, conv_b_w, conv_b_b, ln_b_g, ln_b_b, w_b_out, w_o, g_ffn, w_up, ffn_conv_w, ffn_conv_b, w_down, g_pg, w_pg, w_ple, g_ple, g_final):
    bsz, seq, d = x.shape
    depth = p.shape[0]
    h = x.reshape(bsz * seq, d)
    for i in range(depth):
        h = _mixer_call(h, seq, g_mix[i], w_in[i], ln_v_g[i], ln_v_b[i], w_s[i], b_s[i],
                        w_a_out[i], conv_b_w[i], conv_b_b[i], ln_b_g[i], ln_b_b[i],
                        w_b_out[i], w_o[i])
        h = _ffn_call(h, p[i].reshape(bsz * seq, -1), seq, g_ffn[i], w_up[i], ffn_conv_w[i],
                      ffn_conv_b[i], w_down[i], g_pg[i], w_pg[i], w_ple[i], g_ple[i],
                      g_final, final_norm=(i == depth - 1))
    return h.reshape(bsz, seq, d)
```

```python
import functools
import math

import jax
import jax.numpy as jnp
from jax.experimental import pallas as pl
from jax.experimental.pallas import tpu as pltpu

F32 = jnp.float32
BF16 = jnp.bfloat16

LANES = 128
SUBLANES = 8
MXU_COLS = 256
MIX_ROWS = 256
STEP_ROWS = 512
SUB_ROWS = 256
CHUNK = 128
EPS_RMS = 1e-6
EPS_LN = 1e-5
VMEM_LIMIT_BYTES = 56 * 1024 * 1024

_GELU_C = math.sqrt(2.0 / math.pi)
_LOG2E = math.log2(math.e)


def _pack_rows(w):
    *lead, k, n = w.shape
    pairs = w.astype(BF16).reshape(*lead, k // 2, 2, n)
    return jax.lax.bitcast_convert_type(jnp.swapaxes(pairs, -1, -2), jnp.uint32)


def _unpack_rows(w_u32):
    return pltpu.bitcast(w_u32, BF16)


def _gelu(x):
    inner = x * (_GELU_C + (_GELU_C * 0.044715) * (x * x))
    return (0.5 * x) * (1.0 + jnp.tanh(inner))


def _sigmoid(x):
    return 1.0 / (1.0 + jnp.exp2(x * (-_LOG2E)))


def _rmsnorm(x, g):
    ms = jnp.mean(x * x, axis=-1, keepdims=True)
    return (x * jax.lax.rsqrt(ms + EPS_RMS)) * g


def _layernorm(x, g, b):
    mu = jnp.mean(x, axis=-1, keepdims=True)
    xc = x - mu
    var = jnp.mean(xc * xc, axis=-1, keepdims=True)
    return (xc * jax.lax.rsqrt(var + EPS_LN)) * g + b


def _dot(a, b):
    return jnp.dot(a, b, preferred_element_type=F32)


def _dot_w(a, w_u32):
    return _dot(a, _unpack_rows(w_u32))


def _causal_conv(buf, j, row0, rows, halo, w_ref, b_ref):
    taps = w_ref.shape[0]
    cols = slice(j * LANES, (j + 1) * LANES)
    acc = jnp.zeros((rows, LANES), F32) + b_ref[:, cols]
    for k in range(taps):
        lo = halo + row0 - (taps - 1) + k
        acc = acc + buf[j, lo:lo + rows, :] * w_ref[k:k + 1, cols]
    return acc


def _zeros_after(x):
    bits = jax.lax.bitcast_convert_type(x, jnp.uint32)
    return jax.lax.bitcast_convert_type((bits >> 16) >> 16, F32)


def _order_after(lhs, x):
    rows = 2 * SUBLANES
    folded = x.reshape(x.shape[0] // SUBLANES, SUBLANES, x.shape[1]).sum(axis=0)
    zeros = _zeros_after(jnp.concatenate([folded, folded], axis=0)).astype(lhs.dtype)
    top = jnp.concatenate([lhs[0:rows, 0:LANES] + zeros, lhs[0:rows, LANES:]], axis=1)
    return jnp.concatenate([top, lhs[rows:, :]], axis=0)


def _mixer_step(cur, prev, first_of_seq, x_ref, xprev_ref, g_mix_ref, w_in_ref, ln_v_g_ref,
                ln_v_b_ref, ws_ref, bs_ref, w_a_ref, cw_ref, cb_ref, ln_b_g_ref, ln_b_b_ref,
                w_b_ref, w_o_ref, o_ref, *, halo, width):
    rb = x_ref.shape[0]
    n_lane_blocks = width // LANES
    n_chunks = rb // CHUNK
    glu_c, u_c, vg_c, ga_c, gb_c = cur
    glu_p, u_p, vg_p, ga_p, gb_p = prev

    n_tiles = width // MXU_COLS
    lanes_per_tile = MXU_COLS // LANES
    h = _rmsnorm(x_ref[...], g_mix_ref[...]).astype(BF16)
    v = _layernorm(vg_p[...], ln_v_g_ref[...], ln_v_b_ref[...]).astype(BF16)

    def tile_cols(jb):
        return slice(jb * MXU_COLS, (jb + 1) * MXU_COLS)

    def ztile(lhs, group, jb):
        lo = group * width + jb * MXU_COLS
        return _dot_w(lhs, w_in_ref[:, lo:lo + MXU_COLS])

    def spatial_gating():
        ua_cols = []
        for g in range(n_lane_blocks):
            cols = slice(g * LANES, (g + 1) * LANES)
            rhs = jnp.concatenate(
                [v[n * CHUNK:(n + 1) * CHUNK, cols] for n in range(n_chunks)], axis=1)
            mix = _dot(_unpack_rows(ws_ref[g]), rhs)
            mix = jnp.concatenate(
                [mix[:, n * LANES:(n + 1) * LANES] for n in range(n_chunks)], axis=0)
            bias = jnp.concatenate([bs_ref[g]] * n_chunks, axis=0)
            ua_cols.append(u_p[:, cols] * (mix + bias))
        return jnp.concatenate(ua_cols, axis=1).astype(BF16)

    order = [(grp, jb) for jb in range(n_tiles) for grp in (2, 3)]
    order += [(grp, jb) for grp in (4, 5, 0, 1) for jb in range(n_tiles)]
    conv_before = {3 * j + 3: j for j in range(n_lane_blocks - 1)}
    gating_after = 2 * n_tiles

    conv_cols = []
    lhs = h
    glu_a = None
    ua = None
    for idx, (grp, jb) in enumerate(order):
        if idx in conv_before:
            piece = _causal_conv(glu_p, conv_before[idx], 0, rb, halo, cw_ref, cb_ref)
            conv_cols.append(piece)
            lhs = _order_after(lhs, piece)
        z = ztile(lhs, grp, jb)
        if grp == 2:
            glu_a = z
        elif grp == 3:
            glu = glu_a * _sigmoid(z)
            for q in range(lanes_per_tile):
                glu_c[jb * lanes_per_tile + q, halo:halo + rb, :] = (
                    glu[:, q * LANES:(q + 1) * LANES])
        elif grp == 0:
            u_c[:, tile_cols(jb)] = _gelu(z)
        elif grp == 1:
            vg_c[:, tile_cols(jb)] = _gelu(z)
        elif grp == 4:
            ga_c[:, tile_cols(jb)] = _sigmoid(z)
        else:
            gb_c[:, tile_cols(jb)] = _sigmoid(z)
        if idx == gating_after:
            ua = spatial_gating()

    piece = _causal_conv(glu_p, n_lane_blocks - 1, 0, rb, halo, cw_ref, cb_ref)
    conv_cols.append(piece)
    y_a = _dot_w(_order_after(ua, piece), w_a_ref[...])
    c = _layernorm(jnp.concatenate(conv_cols, axis=1), ln_b_g_ref[...], ln_b_b_ref[...])
    c = (c * _sigmoid(c)).astype(BF16)
    y_b = _dot_w(c, w_b_ref[...])

    for j in range(n_lane_blocks):
        glu_c[j, 0:halo, :] = jnp.where(first_of_seq, 0.0, glu_p[j, rb:rb + halo, :])

    merged = ga_p[...] * y_a + gb_p[...] * y_b
    o_ref[...] = xprev_ref[...] + _dot_w(merged.astype(BF16), w_o_ref[...])


def _mixer_kernel(*refs, blocks_per_seq, halo, width):
    n_in = 15
    in_refs, o_ref, scratch = refs[:n_in], refs[n_in], refs[n_in + 1:]
    slots = (scratch[0::2], scratch[1::2])
    t = pl.program_id(0)

    @pl.when(t == 0)
    def _():
        for ref in scratch:
            ref[...] = jnp.zeros(ref.shape, ref.dtype)

    first_of_seq = t % blocks_per_seq == 0
    for parity in range(2):
        @pl.when(t % 2 == parity)
        def _(parity=parity):
            _mixer_step(slots[parity], slots[1 - parity], first_of_seq, *in_refs, o_ref,
                        halo=halo, width=width)


def _ffn_kernel(x_ref, p_ref, g_ffn_ref, w_up_ref, fw_ref, fb_ref, w_down_ref,
                g_pg_ref, w_pg_ref, w_ple_ref, g_ple_ref, g_final_ref,
                o_ref, ubuf, act, *, steps_per_seq, halo, ffn_dim, final_norm):
    step_rows = x_ref.shape[0]
    n_up_blocks = (2 * ffn_dim) // LANES
    n_gate_blocks = ffn_dim // LANES
    lanes_per_tile = MXU_COLS // LANES

    @pl.when(pl.program_id(0) % steps_per_seq == 0)
    def _():
        ubuf[:, 0:halo, :] = jnp.zeros((n_up_blocks, halo, LANES), F32)

    sub_rows = [slice(r, r + SUB_ROWS) for r in range(0, step_rows, SUB_ROWS)]
    xs, pes = [], []
    for rows in sub_rows:
        x = x_ref[rows, :]
        xs.append(x)
        h = _rmsnorm(x, g_ffn_ref[...]).astype(BF16)
        for jb in range((2 * ffn_dim) // MXU_COLS):
            r = _dot_w(h, w_up_ref[:, jb * MXU_COLS:(jb + 1) * MXU_COLS])
            for q in range(lanes_per_tile):
                ubuf[jb * lanes_per_tile + q, halo + rows.start:halo + rows.stop, :] = (
                    r[:, q * LANES:(q + 1) * LANES])
        pes.append(_rmsnorm(_dot_w(p_ref[rows, :].astype(BF16), w_ple_ref[...]), g_ple_ref[...]))

    for rows in sub_rows:
        for j in range(n_gate_blocks):
            gate = _causal_conv(ubuf, j, rows.start, SUB_ROWS, halo, fw_ref, fb_ref)
            val = _causal_conv(ubuf, j + n_gate_blocks, rows.start, SUB_ROWS, halo, fw_ref, fb_ref)
            act[rows, j * LANES:(j + 1) * LANES] = (_gelu(gate) * val).astype(BF16)
    for j in range(n_up_blocks):
        ubuf[j, 0:halo, :] = ubuf[j, step_rows:step_rows + halo, :]

    xs = [x + _dot_w(act[rows, :], w_down_ref[...]) for x, rows in zip(xs, sub_rows)]
    for x, pe, rows in zip(xs, pes, sub_rows):
        pg = _sigmoid(_dot_w(_rmsnorm(x, g_pg_ref[...]).astype(BF16), w_pg_ref[...]))
        x = x + pe * pg
        if final_norm:
            x = _rmsnorm(x, g_final_ref[...])
        o_ref[rows, :] = x


def _resident(shape):
    zeros = (0,) * len(shape)
    return pl.BlockSpec(shape, lambda i: zeros, pipeline_mode=pl.Buffered(1))


def _row(v):
    return v.reshape(1, -1).astype(F32)


def _halo_rows(taps):
    return -(-(taps - 1) // SUBLANES) * SUBLANES


def _mixer_call(x2d, seq, g_mix, w_in, ln_v_g, ln_v_b, w_s, b_s, w_a_out,
                conv_w, conv_b, ln_b_g, ln_b_b, w_b_out, w_o):
    t, d = x2d.shape
    width = w_a_out.shape[0]
    groups, chunk, _ = w_s.shape
    assert chunk == CHUNK and groups * LANES == width and w_b_out.shape[0] == width
    assert w_in.shape[1] == 6 * width and d == width
    assert seq % MIX_ROWS == 0 and MIX_ROWS % CHUNK == 0
    halo = _halo_rows(conv_w.shape[0])
    n_blocks = t // MIX_ROWS
    tril = jnp.tril(jnp.ones((chunk, chunk), dtype=bool))
    ws = _pack_rows(jnp.where(tril[None], w_s, jnp.zeros_like(w_s)))
    bs = jnp.broadcast_to(b_s[:, :, None], (groups, chunk, LANES)).astype(F32)
    operands = [
        x2d, x2d, _row(g_mix), _pack_rows(w_in), _row(ln_v_g), _row(ln_v_b), ws, bs,
        _pack_rows(w_a_out), conv_w.astype(F32), _row(conv_b), _row(ln_b_g), _row(ln_b_b),
        _pack_rows(w_b_out), _pack_rows(w_o),
    ]
    cur_spec = pl.BlockSpec((MIX_ROWS, d), lambda i: (jnp.minimum(i, n_blocks - 1), 0))
    prev_spec = pl.BlockSpec((MIX_ROWS, d), lambda i: (jnp.maximum(i - 1, 0), 0))
    in_specs = [cur_spec, prev_spec] + [_resident(a.shape) for a in operands[2:]]
    body = functools.partial(_mixer_kernel, blocks_per_seq=seq // MIX_ROWS, halo=halo,
                             width=width)
    per_slot = [
        pltpu.VMEM((width // LANES, halo + MIX_ROWS, LANES), F32),
        pltpu.VMEM((MIX_ROWS, width), F32),
        pltpu.VMEM((MIX_ROWS, width), F32),
        pltpu.VMEM((MIX_ROWS, width), F32),
        pltpu.VMEM((MIX_ROWS, width), F32),
    ]
    return pl.pallas_call(
        body,
        grid=(n_blocks + 1,),
        in_specs=in_specs,
        out_specs=prev_spec,
        out_shape=jax.ShapeDtypeStruct((t, d), F32),
        scratch_shapes=[s for s in per_slot for _ in range(2)],
        compiler_params=pltpu.CompilerParams(
            dimension_semantics=("arbitrary",), vmem_limit_bytes=VMEM_LIMIT_BYTES),
        name="token_mixer",
    )(*operands)


def _ffn_call(x2d, p2d, seq, g_ffn, w_up, ffn_conv_w, ffn_conv_b, w_down,
              g_pg, w_pg, w_ple, g_ple, g_final, final_norm):
    t, d = x2d.shape
    ffn_dim = w_down.shape[0]
    assert w_up.shape[1] == 2 * ffn_dim and ffn_dim % MXU_COLS == 0
    assert seq % STEP_ROWS == 0 and STEP_ROWS % SUB_ROWS == 0
    halo = _halo_rows(ffn_conv_w.shape[0])
    operands = [
        x2d, p2d, _row(g_ffn), _pack_rows(w_up), ffn_conv_w.astype(F32), _row(ffn_conv_b),
        _pack_rows(w_down), _row(g_pg), _pack_rows(w_pg), _pack_rows(w_ple), _row(g_ple),
        _row(g_final),
    ]
    row_spec = pl.BlockSpec((STEP_ROWS, d), lambda i: (i, 0))
    p_spec = pl.BlockSpec((STEP_ROWS, p2d.shape[1]), lambda i: (i, 0))
    in_specs = [row_spec, p_spec] + [_resident(a.shape) for a in operands[2:]]
    body = functools.partial(_ffn_kernel, steps_per_seq=seq // STEP_ROWS, halo=halo,
                             ffn_dim=ffn_dim, final_norm=final_norm)
    return pl.pallas_call(
        body,
        grid=(t // STEP_ROWS,),
        in_specs=in_specs,
        out_specs=row_spec,
        out_shape=jax.ShapeDtypeStruct((t, d), F32),
        scratch_shapes=[
            pltpu.VMEM(((2 * ffn_dim) // LANES, halo + STEP_ROWS, LANES), F32),
            pltpu.VMEM((STEP_ROWS, ffn_dim), BF16),
        ],
        compiler_params=pltpu.CompilerParams(
            dimension_semantics=("arbitrary",), vmem_limit_bytes=VMEM_LIMIT_BYTES),
        name="channel_mixer",
    )(*operands)


def kernel(x, p, g_mix, w_in, ln_v_g, ln_v_b, w_s, b_s, w_a_out, conv_b_w, conv_b_b, ln_b_g, ln_b_b, w_b_out, w_o, g_ffn, w_up, ffn_conv_w, ffn_conv_b, w_down, g_pg, w_pg, w_ple, g_ple, g_final):
    bsz, seq, d = x.shape
    depth = p.shape[0]
    h = x.reshape(bsz * seq, d)
    for i in range(depth):
        h = _mixer_call(h, seq, g_mix[i], w_in[i], ln_v_g[i], ln_v_b[i], w_s[i], b_s[i],
                        w_a_out[i], conv_b_w[i], conv_b_b[i], ln_b_g[i], ln_b_b[i],
                        w_b_out[i], w_o[i])
        h = _ffn_call(h, p[i].reshape(bsz * seq, -1), seq, g_ffn[i], w_up[i], ffn_conv_w[i],
                      ffn_conv_b[i], w_down[i], g_pg[i], w_pg[i], w_ple[i], g_ple[i],
                      g_final, final_norm=(i == depth - 1))
    return h.reshape(bsz, seq, d)
```

```python
import functools
import math

import jax
import jax.numpy as jnp
from jax.experimental import pallas as pl
from jax.experimental.pallas import tpu as pltpu

F32 = jnp.float32
BF16 = jnp.bfloat16

LANES = 128
SUBLANES = 8
MXU_COLS = 256
MIX_ROWS = 256
STEP_ROWS = 512
SUB_ROWS = 256
CHUNK = 128
EPS_RMS = 1e-6
EPS_LN = 1e-5
VMEM_LIMIT_BYTES = 56 * 1024 * 1024

_GELU_C = math.sqrt(2.0 / math.pi)
_LOG2E = math.log2(math.e)


def _pack_rows(w):
    def half(rows):
        bits = jax.lax.bitcast_convert_type(rows.astype(BF16), jnp.uint16)
        return bits.astype(jnp.uint32)

    return half(w[..., 0::2, :]) | (half(w[..., 1::2, :]) << 16)


def _unpack_rows(w_u32):
    return pltpu.bitcast(w_u32, BF16)


def _gelu(x):
    a = -2.0 * _GELU_C * _LOG2E
    t = x * (a + (a * 0.044715) * (x * x))
    return x * (1.0 / (1.0 + jnp.exp2(t)))


def _sigmoid(x):
    return 1.0 / (1.0 + jnp.exp2(x * (-_LOG2E)))


def _rmsnorm(x, g):
    ms = jnp.mean(x * x, axis=-1, keepdims=True)
    return (x * jax.lax.rsqrt(ms + EPS_RMS)) * g


def _layernorm(x, g, b):
    mu = jnp.mean(x, axis=-1, keepdims=True)
    xc = x - mu
    var = jnp.mean(xc * xc, axis=-1, keepdims=True)
    return (xc * jax.lax.rsqrt(var + EPS_LN)) * g + b


def _dot(a, b):
    return jnp.dot(a, b, preferred_element_type=F32)


def _dot_w(a, w_u32):
    return _dot(a, _unpack_rows(w_u32))


def _causal_conv(buf, j, row0, rows, halo, w_ref, b_ref):
    taps = w_ref.shape[0]
    cols = slice(j * LANES, (j + 1) * LANES)
    acc = jnp.broadcast_to(b_ref[:, cols], (rows, LANES))
    for k in range(taps):
        lo = halo + row0 - (taps - 1) + k
        acc = acc + buf[j, lo:lo + rows, :] * w_ref[k:k + 1, cols]
    return acc


def _zeros_after(x):
    bits = jax.lax.bitcast_convert_type(x, jnp.uint32)
    return jax.lax.bitcast_convert_type((bits >> 16) >> 16, F32)


def _order_after(lhs, x):
    rows = 2 * SUBLANES
    folded = x.reshape(x.shape[0] // SUBLANES, SUBLANES, x.shape[1]).sum(axis=0)
    folded = functools.reduce(
        jnp.add, [folded[:, j:j + LANES] for j in range(0, x.shape[1], LANES)])
    zeros = _zeros_after(jnp.concatenate([folded, folded], axis=0)).astype(lhs.dtype)
    top = jnp.concatenate([lhs[0:rows, 0:LANES] + zeros, lhs[0:rows, LANES:]], axis=1)
    return jnp.concatenate([top, lhs[rows:, :]], axis=0)


def _mixer_step(cur, prev, first_of_seq, x_ref, xprev_ref, g_mix_ref, w_in_ref, ln_v_g_ref,
                ln_v_b_ref, ws_ref, bs_ref, w_a_ref, cw_ref, cb_ref, ln_b_g_ref, ln_b_b_ref,
                w_b_ref, w_o_ref, o_ref, *, halo, width):
    rb = x_ref.shape[0]
    n_lane_blocks = width // LANES
    n_chunks = rb // CHUNK
    glu_c, u_c, vg_c, ga_c, gb_c = cur
    glu_p, u_p, vg_p, ga_p, gb_p = prev

    n_tiles = width // MXU_COLS
    lanes_per_tile = MXU_COLS // LANES
    h = _rmsnorm(x_ref[...], g_mix_ref[...]).astype(BF16)
    v = _layernorm(vg_p[...], ln_v_g_ref[...], ln_v_b_ref[...]).astype(BF16)

    def tile_cols(jb):
        return slice(jb * MXU_COLS, (jb + 1) * MXU_COLS)

    def ztile(lhs, group, jb):
        lo = group * width + jb * MXU_COLS
        return _dot_w(lhs, w_in_ref[:, lo:lo + MXU_COLS])

    def spatial_gating():
        ua_cols = []
        for g in range(n_lane_blocks):
            cols = slice(g * LANES, (g + 1) * LANES)
            rhs = jnp.concatenate(
                [v[n * CHUNK:(n + 1) * CHUNK, cols] for n in range(n_chunks)], axis=1)
            mix = _dot(_unpack_rows(ws_ref[g]), rhs)
            mix = jnp.concatenate(
                [mix[:, n * LANES:(n + 1) * LANES] for n in range(n_chunks)], axis=0)
            bias = jnp.concatenate([bs_ref[g]] * n_chunks, axis=0)
            ua_cols.append(u_p[:, cols] * (mix + bias))
        return jnp.concatenate(ua_cols, axis=1).astype(BF16)

    order = [(grp, jb) for jb in range(n_tiles) for grp in (2, 3)]
    order += [(grp, jb) for grp in (4, 5, 0, 1) for jb in range(n_tiles)]
    conv_before = {3 * j + 3: j for j in range(n_lane_blocks - 1)}
    gating_after = 2 * n_tiles

    conv_cols = []
    lhs = h
    glu_a = None
    ua = None
    for idx, (grp, jb) in enumerate(order):
        if idx in conv_before:
            piece = _causal_conv(glu_p, conv_before[idx], 0, rb, halo, cw_ref, cb_ref)
            conv_cols.append(piece)
            lhs = _order_after(lhs, piece)
        z = ztile(lhs, grp, jb)
        if grp == 2:
            glu_a = z
        elif grp == 3:
            glu = glu_a * _sigmoid(z)
            for q in range(lanes_per_tile):
                glu_c[jb * lanes_per_tile + q, halo:halo + rb, :] = (
                    glu[:, q * LANES:(q + 1) * LANES])
        elif grp == 0:
            u_c[:, tile_cols(jb)] = _gelu(z)
        elif grp == 1:
            vg_c[:, tile_cols(jb)] = _gelu(z)
        elif grp == 4:
            ga_c[:, tile_cols(jb)] = _sigmoid(z)
        else:
            gb_c[:, tile_cols(jb)] = _sigmoid(z)
        if idx == gating_after:
            ua = spatial_gating()

    piece = _causal_conv(glu_p, n_lane_blocks - 1, 0, rb, halo, cw_ref, cb_ref)
    conv_cols.append(piece)
    y_a = _dot_w(_order_after(ua, piece), w_a_ref[...])
    c = _layernorm(jnp.concatenate(conv_cols, axis=1), ln_b_g_ref[...], ln_b_b_ref[...])
    c = (c * _sigmoid(c)).astype(BF16)
    y_b = _dot_w(c, w_b_ref[...])

    for j in range(n_lane_blocks):
        glu_c[j, 0:halo, :] = jnp.where(first_of_seq, 0.0, glu_p[j, rb:rb + halo, :])

    merged = ga_p[...] * y_a + gb_p[...] * y_b
    o_ref[...] = xprev_ref[...] + _dot_w(merged.astype(BF16), w_o_ref[...])


def _mixer_kernel(*refs, blocks_per_seq, halo, width):
    n_in = 15
    in_refs, o_ref, scratch = refs[:n_in], refs[n_in], refs[n_in + 1:]
    slots = (scratch[0::2], scratch[1::2])
    t = pl.program_id(0)

    @pl.when(t == 0)
    def _():
        for ref in scratch:
            ref[...] = jnp.zeros(ref.shape, ref.dtype)

    first_of_seq = t % blocks_per_seq == 0
    for parity in range(2):
        @pl.when(t % 2 == parity)
        def _(parity=parity):
            _mixer_step(slots[parity], slots[1 - parity], first_of_seq, *in_refs, o_ref,
                        halo=halo, width=width)


def _ffn_kernel(x_ref, p_ref, g_ffn_ref, w_up_ref, fw_ref, fb_ref, w_down_ref,
                g_pg_ref, w_pg_ref, w_ple_ref, g_ple_ref, g_final_ref,
                o_ref, ubuf, act, *, steps_per_seq, halo, ffn_dim, final_norm):
    step_rows = x_ref.shape[0]
    n_up_blocks = (2 * ffn_dim) // LANES
    n_gate_blocks = ffn_dim // LANES
    lanes_per_tile = MXU_COLS // LANES

    @pl.when(pl.program_id(0) % steps_per_seq == 0)
    def _():
        ubuf[:, 0:halo, :] = jnp.zeros((n_up_blocks, halo, LANES), F32)

    sub_rows = [slice(r, r + SUB_ROWS) for r in range(0, step_rows, SUB_ROWS)]
    pes = [_rmsnorm(_dot_w(p_ref[rows, :].astype(BF16), w_ple_ref[...]), g_ple_ref[...])
           for rows in sub_rows]
    xs = []
    for rows in sub_rows:
        x = x_ref[rows, :]
        xs.append(x)
        h = _rmsnorm(x, g_ffn_ref[...]).astype(BF16)
        for jb in range((2 * ffn_dim) // MXU_COLS):
            r = _dot_w(h, w_up_ref[:, jb * MXU_COLS:(jb + 1) * MXU_COLS])
            for q in range(lanes_per_tile):
                ubuf[jb * lanes_per_tile + q, halo + rows.start:halo + rows.stop, :] = (
                    r[:, q * LANES:(q + 1) * LANES])

    for rows in sub_rows:
        for j in range(n_gate_blocks):
            gate = _causal_conv(ubuf, j, rows.start, SUB_ROWS, halo, fw_ref, fb_ref)
            val = _causal_conv(ubuf, j + n_gate_blocks, rows.start, SUB_ROWS, halo, fw_ref, fb_ref)
            act[rows, j * LANES:(j + 1) * LANES] = (_gelu(gate) * val).astype(BF16)
    for j in range(n_up_blocks):
        ubuf[j, 0:halo, :] = ubuf[j, step_rows:step_rows + halo, :]

    xs = [x + _dot_w(act[rows, :], w_down_ref[...]) for x, rows in zip(xs, sub_rows)]
    for x, pe, rows in zip(xs, pes, sub_rows):
        pg = _sigmoid(_dot_w(_rmsnorm(x, g_pg_ref[...]).astype(BF16), w_pg_ref[...]))
        x = x + pe * pg
        if final_norm:
            x = _rmsnorm(x, g_final_ref[...])
        o_ref[rows, :] = x


def _resident(shape):
    zeros = (0,) * len(shape)
    return pl.BlockSpec(shape, lambda i: zeros, pipeline_mode=pl.Buffered(1))


def _row(v):
    return v.reshape(1, -1).astype(F32)


def _halo_rows(taps):
    return -(-(taps - 1) // SUBLANES) * SUBLANES


def _mixer_call(x2d, seq, g_mix, w_in, ln_v_g, ln_v_b, w_s, b_s, w_a_out,
                conv_w, conv_b, ln_b_g, ln_b_b, w_b_out, w_o):
    t, d = x2d.shape
    width = w_a_out.shape[0]
    groups, chunk, _ = w_s.shape
    assert chunk == CHUNK and groups * LANES == width and w_b_out.shape[0] == width
    assert w_in.shape[1] == 6 * width and d == width
    assert seq % MIX_ROWS == 0 and MIX_ROWS % CHUNK == 0
    halo = _halo_rows(conv_w.shape[0])
    n_blocks = t // MIX_ROWS
    tril = jnp.tril(jnp.ones((chunk, chunk), dtype=bool))
    ws = _pack_rows(jnp.where(tril[None], w_s, jnp.zeros_like(w_s)))
    bs = jnp.broadcast_to(b_s[:, :, None], (groups, chunk, LANES)).astype(F32)
    operands = [
        x2d, x2d, _row(g_mix), _pack_rows(w_in), _row(ln_v_g), _row(ln_v_b), ws, bs,
        _pack_rows(w_a_out), conv_w.astype(F32), _row(conv_b), _row(ln_b_g), _row(ln_b_b),
        _pack_rows(w_b_out), _pack_rows(w_o),
    ]
    cur_spec = pl.BlockSpec((MIX_ROWS, d), lambda i: (jnp.minimum(i, n_blocks - 1), 0))
    prev_spec = pl.BlockSpec((MIX_ROWS, d), lambda i: (jnp.maximum(i - 1, 0), 0))
    in_specs = [cur_spec, prev_spec] + [_resident(a.shape) for a in operands[2:]]
    body = functools.partial(_mixer_kernel, blocks_per_seq=seq // MIX_ROWS, halo=halo,
                             width=width)
    per_slot = [
        pltpu.VMEM((width // LANES, halo + MIX_ROWS, LANES), F32),
        pltpu.VMEM((MIX_ROWS, width), F32),
        pltpu.VMEM((MIX_ROWS, width), F32),
        pltpu.VMEM((MIX_ROWS, width), F32),
        pltpu.VMEM((MIX_ROWS, width), F32),
    ]
    return pl.pallas_call(
        body,
        grid=(n_blocks + 1,),
        in_specs=in_specs,
        out_specs=prev_spec,
        out_shape=jax.ShapeDtypeStruct((t, d), F32),
        scratch_shapes=[s for s in per_slot for _ in range(2)],
        compiler_params=pltpu.CompilerParams(
            dimension_semantics=("arbitrary",), vmem_limit_bytes=VMEM_LIMIT_BYTES),
        name="token_mixer",
    )(*operands)


def _ffn_call(x2d, p2d, seq, g_ffn, w_up, ffn_conv_w, ffn_conv_b, w_down,
              g_pg, w_pg, w_ple, g_ple, g_final, final_norm):
    t, d = x2d.shape
    ffn_dim = w_down.shape[0]
    assert w_up.shape[1] == 2 * ffn_dim and ffn_dim % MXU_COLS == 0
    assert seq % STEP_ROWS == 0 and STEP_ROWS % SUB_ROWS == 0
    halo = _halo_rows(ffn_conv_w.shape[0])
    operands = [
        x2d, p2d, _row(g_ffn), _pack_rows(w_up), ffn_conv_w.astype(F32), _row(ffn_conv_b),
        _pack_rows(w_down), _row(g_pg), _pack_rows(w_pg), _pack_rows(w_ple), _row(g_ple),
        _row(g_final),
    ]
    row_spec = pl.BlockSpec((STEP_ROWS, d), lambda i: (i, 0))
    p_spec = pl.BlockSpec((STEP_ROWS, p2d.shape[1]), lambda i: (i, 0))
    in_specs = [row_spec, p_spec] + [_resident(a.shape) for a in operands[2:]]
    body = functools.partial(_ffn_kernel, steps_per_seq=seq // STEP_ROWS, halo=halo,
                             ffn_dim=ffn_dim, final_norm=final_norm)
    return pl.pallas_call(
        body,
        grid=(t // STEP_ROWS,),
        in_specs=in_specs,
        out_specs=row_spec,
        out_shape=jax.ShapeDtypeStruct((t, d), F32),
        scratch_shapes=[
            pltpu.VMEM(((2 * ffn_dim) // LANES, halo + STEP_ROWS, LANES), F32),
            pltpu.VMEM((STEP_ROWS, ffn_dim), BF16),
        ],
        compiler_params=pltpu.CompilerParams(
            dimension_semantics=("arbitrary",), vmem_limit_bytes=VMEM_LIMIT_BYTES),
        name="channel_mixer",
    )(*operands)


def kernel(x, p, g_mix, w_in, ln_v_g, ln_v_b, w_s, b_s, w_a_out, conv_b_w, conv_b_b, ln_b_g, ln_b_b, w_b_out, w_o, g_ffn, w_up, ffn_conv_w, ffn_conv_b, w_down, g_pg, w_pg, w_ple, g_ple, g_final):
    bsz, seq, d = x.shape
    depth = p.shape[0]
    h = x.reshape(bsz * seq, d)
    for i in range(depth):
        h = _mixer_call(h, seq, g_mix[i], w_in[i], ln_v_g[i], ln_v_b[i], w_s[i], b_s[i],
                        w_a_out[i], conv_b_w[i], conv_b_b[i], ln_b_g[i], ln_b_b[i],
                        w_b_out[i], w_o[i])
        h = _ffn_call(h, p[i].reshape(bsz * seq, -1), seq, g_ffn[i], w_up[i], ffn_conv_w[i],
                      ffn_conv_b[i], w_down[i], g_pg[i], w_pg[i], w_ple[i], g_ple[i],
                      g_final, final_norm=(i == depth - 1))
    return h.reshape(bsz, seq, d)
```

```python
import functools
import math

import jax
import jax.numpy as jnp
from jax.experimental import pallas as pl
from jax.experimental.pallas import tpu as pltpu

F32 = jnp.float32
BF16 = jnp.bfloat16

LANES = 128
SUBLANES = 8
MXU_COLS = 256
MIX_ROWS = 256
STEP_ROWS = 512
SUB_ROWS = 256
PACK_ROWS = 512
PACK_COLS = 2048
CHUNK = 128
EPS_RMS = 1e-6
EPS_LN = 1e-5
VMEM_LIMIT_BYTES = 56 * 1024 * 1024

_GELU_C = math.sqrt(2.0 / math.pi)
_LOG2E = math.log2(math.e)


def _largest_block(size, cap, unit):
    return max(b for b in range(unit, min(size, cap) + 1, unit) if size % b == 0)


def _pack_rows(w):
    *lead, k, n = w.shape
    w2d = w.reshape(-1, n)
    assert k % 2 == 0
    rows = _largest_block(w2d.shape[0], PACK_ROWS, 2 * SUBLANES)
    cols = _largest_block(n, PACK_COLS, LANES)
    packed = pl.pallas_call(
        _pack_kernel,
        grid=(w2d.shape[0] // rows, n // cols),
        in_specs=[pl.BlockSpec((rows, cols), lambda i, j: (i, j))],
        out_specs=pl.BlockSpec((rows // 2, cols), lambda i, j: (i, j)),
        out_shape=jax.ShapeDtypeStruct((w2d.shape[0] // 2, n), jnp.uint32),
        name="pack_weight",
    )(w2d)
    return packed.reshape(*lead, k // 2, n)


def _pack_kernel(w_ref, o_ref):
    o_ref[...] = pltpu.bitcast(w_ref[...].astype(BF16), jnp.uint32)


def _unpack_rows(w_u32):
    return pltpu.bitcast(w_u32, BF16)


def _gelu(x):
    inner = x * (_GELU_C + (_GELU_C * 0.044715) * (x * x))
    return (0.5 * x) * (1.0 + jnp.tanh(inner))


def _sigmoid(x):
    return 1.0 / (1.0 + jnp.exp2(x * (-_LOG2E)))


def _rmsnorm(x, g):
    ms = jnp.mean(x * x, axis=-1, keepdims=True)
    return (x * jax.lax.rsqrt(ms + EPS_RMS)) * g


def _layernorm(x, g, b):
    mu = jnp.mean(x, axis=-1, keepdims=True)
    xc = x - mu
    var = jnp.mean(xc * xc, axis=-1, keepdims=True)
    return (xc * jax.lax.rsqrt(var + EPS_LN)) * g + b


def _dot(a, b):
    return jnp.dot(a, b, preferred_element_type=F32)


def _dot_w(a, w_u32):
    return _dot(a, _unpack_rows(w_u32))


def _causal_conv(buf, j, row0, rows, halo, w_ref, b_ref):
    taps = w_ref.shape[0]
    cols = slice(j * LANES, (j + 1) * LANES)
    acc = jnp.zeros((rows, LANES), F32) + b_ref[:, cols]
    for k in range(taps):
        lo = halo + row0 - (taps - 1) + k
        acc = acc + buf[j, lo:lo + rows, :] * w_ref[k:k + 1, cols]
    return acc


def _zeros_after(x):
    bits = jax.lax.bitcast_convert_type(x, jnp.uint32)
    return jax.lax.bitcast_convert_type((bits >> 16) >> 16, F32)


def _order_after(lhs, x):
    rows = 2 * SUBLANES
    folded = x.reshape(x.shape[0] // SUBLANES, SUBLANES, x.shape[1]).sum(axis=0)
    folded = functools.reduce(
        jnp.add, [folded[:, j:j + LANES] for j in range(0, x.shape[1], LANES)])
    zeros = _zeros_after(jnp.concatenate([folded, folded], axis=0)).astype(lhs.dtype)
    top = jnp.concatenate([lhs[0:rows, 0:LANES] + zeros, lhs[0:rows, LANES:]], axis=1)
    return jnp.concatenate([top, lhs[rows:, :]], axis=0)


def _mixer_step(cur, prev, first_of_seq, x_ref, xprev_ref, g_mix_ref, w_in_ref, ln_v_g_ref,
                ln_v_b_ref, ws_ref, bs_ref, w_a_ref, cw_ref, cb_ref, ln_b_g_ref, ln_b_b_ref,
                w_b_ref, w_o_ref, o_ref, *, halo, width):
    rb = x_ref.shape[0]
    n_lane_blocks = width // LANES
    n_chunks = rb // CHUNK
    glu_c, u_c, vg_c, ga_c, gb_c = cur
    glu_p, u_p, vg_p, ga_p, gb_p = prev

    n_tiles = width // MXU_COLS
    lanes_per_tile = MXU_COLS // LANES
    h = _rmsnorm(x_ref[...], g_mix_ref[...]).astype(BF16)
    v = _layernorm(vg_p[...], ln_v_g_ref[...], ln_v_b_ref[...]).astype(BF16)

    def tile_cols(jb):
        return slice(jb * MXU_COLS, (jb + 1) * MXU_COLS)

    def ztile(lhs, group, jb):
        lo = group * width + jb * MXU_COLS
        return _dot_w(lhs, w_in_ref[:, lo:lo + MXU_COLS])

    def spatial_gating():
        ua_cols = []
        for g in range(n_lane_blocks):
            cols = slice(g * LANES, (g + 1) * LANES)
            rhs = jnp.concatenate(
                [v[n * CHUNK:(n + 1) * CHUNK, cols] for n in range(n_chunks)], axis=1)
            mix = _dot(_unpack_rows(ws_ref[g]), rhs)
            mix = jnp.concatenate(
                [mix[:, n * LANES:(n + 1) * LANES] for n in range(n_chunks)], axis=0)
            bias = jnp.concatenate([bs_ref[g]] * n_chunks, axis=0)
            ua_cols.append(u_p[:, cols] * (mix + bias))
        return jnp.concatenate(ua_cols, axis=1).astype(BF16)

    order = [(grp, jb) for jb in range(n_tiles) for grp in (2, 3)]
    order += [(grp, jb) for grp in (4, 5, 0, 1) for jb in range(n_tiles)]
    conv_before = {3 * j + 3: j for j in range(n_lane_blocks - 1)}
    gating_after = 2 * n_tiles

    conv_cols = []
    lhs = h
    glu_a = None
    ua = None
    for idx, (grp, jb) in enumerate(order):
        if idx in conv_before:
            piece = _causal_conv(glu_p, conv_before[idx], 0, rb, halo, cw_ref, cb_ref)
            conv_cols.append(piece)
            lhs = _order_after(lhs, piece)
        z = ztile(lhs, grp, jb)
        if grp == 2:
            glu_a = z
        elif grp == 3:
            glu = glu_a * _sigmoid(z)
            for q in range(lanes_per_tile):
                glu_c[jb * lanes_per_tile + q, halo:halo + rb, :] = (
                    glu[:, q * LANES:(q + 1) * LANES])
        elif grp == 0:
            u_c[:, tile_cols(jb)] = _gelu(z)
        elif grp == 1:
            vg_c[:, tile_cols(jb)] = _gelu(z)
        elif grp == 4:
            ga_c[:, tile_cols(jb)] = _sigmoid(z)
        else:
            gb_c[:, tile_cols(jb)] = _sigmoid(z)
        if idx == gating_after:
            ua = spatial_gating()

    piece = _causal_conv(glu_p, n_lane_blocks - 1, 0, rb, halo, cw_ref, cb_ref)
    conv_cols.append(piece)
    y_a = _dot_w(_order_after(ua, piece), w_a_ref[...])
    c = _layernorm(jnp.concatenate(conv_cols, axis=1), ln_b_g_ref[...], ln_b_b_ref[...])
    c = (c * _sigmoid(c)).astype(BF16)
    y_b = _dot_w(c, w_b_ref[...])

    for j in range(n_lane_blocks):
        glu_c[j, 0:halo, :] = jnp.where(first_of_seq, 0.0, glu_p[j, rb:rb + halo, :])

    merged = ga_p[...] * y_a + gb_p[...] * y_b
    o_ref[...] = xprev_ref[...] + _dot_w(merged.astype(BF16), w_o_ref[...])


def _mixer_kernel(*refs, blocks_per_seq, halo, width):
    n_in = 15
    in_refs, o_ref, scratch = refs[:n_in], refs[n_in], refs[n_in + 1:]
    slots = (scratch[0::2], scratch[1::2])
    t = pl.program_id(0)

    @pl.when(t == 0)
    def _():
        for ref in scratch:
            ref[...] = jnp.zeros(ref.shape, ref.dtype)

    first_of_seq = t % blocks_per_seq == 0
    for parity in range(2):
        @pl.when(t % 2 == parity)
        def _(parity=parity):
            _mixer_step(slots[parity], slots[1 - parity], first_of_seq, *in_refs, o_ref,
                        halo=halo, width=width)


def _ffn_kernel(x_ref, p_ref, g_ffn_ref, w_up_ref, fw_ref, fb_ref, w_down_ref,
                g_pg_ref, w_pg_ref, w_ple_ref, g_ple_ref, g_final_ref,
                o_ref, ubuf, act, *, steps_per_seq, halo, ffn_dim, final_norm):
    step_rows = x_ref.shape[0]
    n_up_blocks = (2 * ffn_dim) // LANES
    n_gate_blocks = ffn_dim // LANES
    lanes_per_tile = MXU_COLS // LANES

    @pl.when(pl.program_id(0) % steps_per_seq == 0)
    def _():
        ubuf[:, 0:halo, :] = jnp.zeros((n_up_blocks, halo, LANES), F32)

    sub_rows = [slice(r, r + SUB_ROWS) for r in range(0, step_rows, SUB_ROWS)]
    xs, pes = [], []
    for rows in sub_rows:
        x = x_ref[rows, :]
        xs.append(x)
        h = _rmsnorm(x, g_ffn_ref[...]).astype(BF16)
        for jb in range((2 * ffn_dim) // MXU_COLS):
            r = _dot_w(h, w_up_ref[:, jb * MXU_COLS:(jb + 1) * MXU_COLS])
            for q in range(lanes_per_tile):
                ubuf[jb * lanes_per_tile + q, halo + rows.start:halo + rows.stop, :] = (
                    r[:, q * LANES:(q + 1) * LANES])
        pes.append(_rmsnorm(_dot_w(p_ref[rows, :].astype(BF16), w_ple_ref[...]), g_ple_ref[...]))

    for rows in sub_rows:
        for j in range(n_gate_blocks):
            gate = _causal_conv(ubuf, j, rows.start, SUB_ROWS, halo, fw_ref, fb_ref)
            val = _causal_conv(ubuf, j + n_gate_blocks, rows.start, SUB_ROWS, halo, fw_ref, fb_ref)
            act[rows, j * LANES:(j + 1) * LANES] = (_gelu(gate) * val).astype(BF16)
    for j in range(n_up_blocks):
        ubuf[j, 0:halo, :] = ubuf[j, step_rows:step_rows + halo, :]

    xs = [x + _dot_w(act[rows, :], w_down_ref[...]) for x, rows in zip(xs, sub_rows)]
    for x, pe, rows in zip(xs, pes, sub_rows):
        pg = _sigmoid(_dot_w(_rmsnorm(x, g_pg_ref[...]).astype(BF16), w_pg_ref[...]))
        x = x + pe * pg
        if final_norm:
            x = _rmsnorm(x, g_final_ref[...])
        o_ref[rows, :] = x


def _resident(shape):
    zeros = (0,) * len(shape)
    return pl.BlockSpec(shape, lambda i: zeros, pipeline_mode=pl.Buffered(1))


def _row(v):
    return v.reshape(1, -1).astype(F32)


def _halo_rows(taps):
    return -(-(taps - 1) // SUBLANES) * SUBLANES


def _mixer_call(x2d, seq, g_mix, w_in, ln_v_g, ln_v_b, w_s, b_s, w_a_out,
                conv_w, conv_b, ln_b_g, ln_b_b, w_b_out, w_o):
    t, d = x2d.shape
    width = w_a_out.shape[0]
    groups, chunk, _ = w_s.shape
    assert chunk == CHUNK and groups * LANES == width and w_b_out.shape[0] == width
    assert w_in.shape[1] == 6 * width and d == width
    assert seq % MIX_ROWS == 0 and MIX_ROWS % CHUNK == 0
    halo = _halo_rows(conv_w.shape[0])
    n_blocks = t // MIX_ROWS
    tril = jnp.tril(jnp.ones((chunk, chunk), dtype=bool))
    ws = _pack_rows(jnp.where(tril[None], w_s, jnp.zeros_like(w_s)))
    bs = jnp.broadcast_to(b_s[:, :, None], (groups, chunk, LANES)).astype(F32)
    operands = [
        x2d, x2d, _row(g_mix), _pack_rows(w_in), _row(ln_v_g), _row(ln_v_b), ws, bs,
        _pack_rows(w_a_out), conv_w.astype(F32), _row(conv_b), _row(ln_b_g), _row(ln_b_b),
        _pack_rows(w_b_out), _pack_rows(w_o),
    ]
    cur_spec = pl.BlockSpec((MIX_ROWS, d), lambda i: (jnp.minimum(i, n_blocks - 1), 0))
    prev_spec = pl.BlockSpec((MIX_ROWS, d), lambda i: (jnp.maximum(i - 1, 0), 0))
    in_specs = [cur_spec, prev_spec] + [_resident(a.shape) for a in operands[2:]]
    body = functools.partial(_mixer_kernel, blocks_per_seq=seq // MIX_ROWS, halo=halo,
                             width=width)
    per_slot = [
        pltpu.VMEM((width // LANES, halo + MIX_ROWS, LANES), F32),
        pltpu.VMEM((MIX_ROWS, width), F32),
        pltpu.VMEM((MIX_ROWS, width), F32),
        pltpu.VMEM((MIX_ROWS, width), F32),
        pltpu.VMEM((MIX_ROWS, width), F32),
    ]
    return pl.pallas_call(
        body,
        grid=(n_blocks + 1,),
        in_specs=in_specs,
        out_specs=prev_spec,
        out_shape=jax.ShapeDtypeStruct((t, d), F32),
        scratch_shapes=[s for s in per_slot for _ in range(2)],
        compiler_params=pltpu.CompilerParams(
            dimension_semantics=("arbitrary",), vmem_limit_bytes=VMEM_LIMIT_BYTES),
        name="token_mixer",
    )(*operands)


def _ffn_call(x2d, p2d, seq, g_ffn, w_up, ffn_conv_w, ffn_conv_b, w_down,
              g_pg, w_pg, w_ple, g_ple, g_final, final_norm):
    t, d = x2d.shape
    ffn_dim = w_down.shape[0]
    assert w_up.shape[1] == 2 * ffn_dim and ffn_dim % MXU_COLS == 0
    assert seq % STEP_ROWS == 0 and STEP_ROWS % SUB_ROWS == 0
    halo = _halo_rows(ffn_conv_w.shape[0])
    operands = [
        x2d, p2d, _row(g_ffn), _pack_rows(w_up), ffn_conv_w.astype(F32), _row(ffn_conv_b),
        _pack_rows(w_down), _row(g_pg), _pack_rows(w_pg), _pack_rows(w_ple), _row(g_ple),
        _row(g_final),
    ]
    row_spec = pl.BlockSpec((STEP_ROWS, d), lambda i: (i, 0))
    p_spec = pl.BlockSpec((STEP_ROWS, p2d.shape[1]), lambda i: (i, 0))
    in_specs = [row_spec, p_spec] + [_resident(a.shape) for a in operands[2:]]
    body = functools.partial(_ffn_kernel, steps_per_seq=seq // STEP_ROWS, halo=halo,
                             ffn_dim=ffn_dim, final_norm=final_norm)
    return pl.pallas_call(
        body,
        grid=(t // STEP_ROWS,),
        in_specs=in_specs,
        out_specs=row_spec,
        out_shape=jax.ShapeDtypeStruct((t, d), F32),
        scratch_shapes=[
            pltpu.VMEM(((2 * ffn_dim) // LANES, halo + STEP_ROWS, LANES), F32),
            pltpu.VMEM((STEP_ROWS, ffn_dim), BF16),
        ],
        compiler_params=pltpu.CompilerParams(
            dimension_semantics=("arbitrary",), vmem_limit_bytes=VMEM_LIMIT_BYTES),
        name="channel_mixer",
    )(*operands)


def kernel(x, p, g_mix, w_in, ln_v_g, ln_v_b, w_s, b_s, w_a_out, conv_b_w, conv_b_b, ln_b_g, ln_b_b, w_b_out, w_o, g_ffn, w_up, ffn_conv_w, ffn_conv_b, w_down, g_pg, w_pg, w_ple, g_ple, g_final):
    bsz, seq, d = x.shape
    depth = p.shape[0]
    h = x.reshape(bsz * seq, d)
    for i in range(depth):
        h = _mixer_call(h, seq, g_mix[i], w_in[i], ln_v_g[i], ln_v_b[i], w_s[i], b_s[i],
                        w_a_out[i], conv_b_w[i], conv_b_b[i], ln_b_g[i], ln_b_b[i],
                        w_b_out[i], w_o[i])
        h = _ffn_call(h, p[i].reshape(bsz * seq, -1), seq, g_ffn[i], w_up[i], ffn_conv_w[i],
                      ffn_conv_b[i], w_down[i], g_pg[i], w_pg[i], w_ple[i], g_ple[i],
                      g_final, final_norm=(i == depth - 1))
    return h.reshape(bsz, seq, d)
```

```python
import functools
import math

import jax
import jax.numpy as jnp
from jax.experimental import pallas as pl
from jax.experimental.pallas import tpu as pltpu

F32 = jnp.float32
BF16 = jnp.bfloat16

LANES = 128
SUBLANES = 8
MXU_COLS = 256
MIX_ROWS = 256
STEP_ROWS = 512
SUB_ROWS = 256
PACK_ROWS = 512
PACK_COLS = 2048
CHUNK = 128
CONV_GROUP = 4
EPS_RMS = 1e-6
EPS_LN = 1e-5
VMEM_LIMIT_BYTES = 56 * 1024 * 1024

_GELU_C = math.sqrt(2.0 / math.pi)
_LOG2E = math.log2(math.e)


def _largest_block(size, cap, unit):
    return max(b for b in range(unit, min(size, cap) + 1, unit) if size % b == 0)


def _pack_rows(w):
    *lead, k, n = w.shape
    w2d = w.reshape(-1, n)
    assert k % 2 == 0
    rows = _largest_block(w2d.shape[0], PACK_ROWS, 2 * SUBLANES)
    cols = _largest_block(n, PACK_COLS, LANES)
    packed = pl.pallas_call(
        _pack_kernel,
        grid=(w2d.shape[0] // rows, n // cols),
        in_specs=[pl.BlockSpec((rows, cols), lambda i, j: (i, j))],
        out_specs=pl.BlockSpec((rows // 2, cols), lambda i, j: (i, j)),
        out_shape=jax.ShapeDtypeStruct((w2d.shape[0] // 2, n), jnp.uint32),
        name="pack_weight",
    )(w2d)
    return packed.reshape(*lead, k // 2, n)


def _pack_kernel(w_ref, o_ref):
    o_ref[...] = pltpu.bitcast(w_ref[...].astype(BF16), jnp.uint32)


def _unpack_rows(w_u32):
    return pltpu.bitcast(w_u32, BF16)


def _gelu(x):
    inner = x * (_GELU_C + (_GELU_C * 0.044715) * (x * x))
    return (0.5 * x) * (1.0 + jnp.tanh(inner))


def _sigmoid(x):
    return 1.0 / (1.0 + jnp.exp2(x * (-_LOG2E)))


def _rmsnorm(x, g):
    ms = jnp.mean(x * x, axis=-1, keepdims=True)
    return (x * jax.lax.rsqrt(ms + EPS_RMS)) * g


def _layernorm(x, g, b):
    mu = jnp.mean(x, axis=-1, keepdims=True)
    xc = x - mu
    var = jnp.mean(xc * xc, axis=-1, keepdims=True)
    return (xc * jax.lax.rsqrt(var + EPS_LN)) * g + b


def _dot(a, b):
    return jnp.dot(a, b, preferred_element_type=F32)


def _dot_w(a, w_u32):
    return _dot(a, _unpack_rows(w_u32))


def _causal_conv(buf, j, row0, rows, halo, w_ref, b_ref):
    taps = w_ref.shape[0]
    cols = slice(j * LANES, (j + 1) * LANES)
    acc = jnp.zeros((rows, LANES), F32) + b_ref[:, cols]
    for k in range(taps):
        lo = halo + row0 - (taps - 1) + k
        acc = acc + buf[j, lo:lo + rows, :] * w_ref[k:k + 1, cols]
    return acc


def _causal_conv_packed(even_ref, odd_ref, j, rows, halo, w_ref, b_ref):
    taps = w_ref.shape[0]
    cols = slice(j * LANES, (j + 1) * LANES)
    words = rows // 2
    acc = jnp.zeros((rows, LANES), F32) + b_ref[:, cols]
    group = None
    for k in range(taps):
        start = halo - (taps - 1) + k
        src = odd_ref if start % 2 else even_ref
        win = pltpu.bitcast(src[j, start // 2:start // 2 + words, :], BF16)
        w_k = pltpu.bitcast(jnp.broadcast_to(w_ref[k:k + 1, cols], (words, LANES)), BF16)
        group = win * w_k if group is None else group + win * w_k
        if (k + 1) % CONV_GROUP == 0 or k == taps - 1:
            acc = acc + group.astype(F32)
            group = None
    return acc


def _zeros_after(x):
    bits = jax.lax.bitcast_convert_type(x, jnp.uint32)
    return jax.lax.bitcast_convert_type((bits >> 16) >> 16, F32)


def _order_after(lhs, x):
    rows = 2 * SUBLANES
    folded = x.reshape(x.shape[0] // SUBLANES, SUBLANES, x.shape[1]).sum(axis=0)
    folded = functools.reduce(
        jnp.add, [folded[:, j:j + LANES] for j in range(0, x.shape[1], LANES)])
    zeros = _zeros_after(jnp.concatenate([folded, folded], axis=0)).astype(lhs.dtype)
    top = jnp.concatenate([lhs[0:rows, 0:LANES] + zeros, lhs[0:rows, LANES:]], axis=1)
    return jnp.concatenate([top, lhs[rows:, :]], axis=0)


def _mixer_step(cur, prev, first_of_seq, x_ref, xprev_ref, g_mix_ref, w_in_ref, ln_v_g_ref,
                ln_v_b_ref, ws_ref, bs_ref, w_a_ref, cw_ref, cb_ref, ln_b_g_ref, ln_b_b_ref,
                w_b_ref, w_o_ref, o_ref, *, halo, width):
    rb = x_ref.shape[0]
    n_lane_blocks = width // LANES
    n_chunks = rb // CHUNK
    glu_c, u_c, vg_c, ga_c, gb_c, even_c, odd_c = cur
    glu_p, u_p, vg_p, ga_p, gb_p, even_p, odd_p = prev

    n_tiles = width // MXU_COLS
    lanes_per_tile = MXU_COLS // LANES
    h = _rmsnorm(x_ref[...], g_mix_ref[...]).astype(BF16)
    v = _layernorm(vg_p[...], ln_v_g_ref[...], ln_v_b_ref[...]).astype(BF16)

    def tile_cols(jb):
        return slice(jb * MXU_COLS, (jb + 1) * MXU_COLS)

    def ztile(lhs, group, jb):
        lo = group * width + jb * MXU_COLS
        return _dot_w(lhs, w_in_ref[:, lo:lo + MXU_COLS])

    def spatial_gating():
        ua_cols = []
        for g in range(n_lane_blocks):
            cols = slice(g * LANES, (g + 1) * LANES)
            rhs = jnp.concatenate(
                [v[n * CHUNK:(n + 1) * CHUNK, cols] for n in range(n_chunks)], axis=1)
            mix = _dot(_unpack_rows(ws_ref[g]), rhs)
            mix = jnp.concatenate(
                [mix[:, n * LANES:(n + 1) * LANES] for n in range(n_chunks)], axis=0)
            bias = jnp.concatenate([bs_ref[g]] * n_chunks, axis=0)
            ua_cols.append(u_p[:, cols] * (mix + bias))
        return jnp.concatenate(ua_cols, axis=1).astype(BF16)

    order = [(grp, jb) for jb in range(n_tiles) for grp in (2, 3)]
    order += [(grp, jb) for grp in (4, 5, 0, 1) for jb in range(n_tiles)]
    conv_before = {3 * j + 3: j for j in range(n_lane_blocks - 1)}
    gating_after = 2 * n_tiles

    conv_cols = []
    lhs = h
    glu_a = None
    ua = None
    for idx, (grp, jb) in enumerate(order):
        if idx in conv_before:
            piece = _causal_conv_packed(
                even_p, odd_p, conv_before[idx], rb, halo, cw_ref, cb_ref)
            conv_cols.append(piece)
            lhs = _order_after(lhs, piece)
        z = ztile(lhs, grp, jb)
        if grp == 2:
            glu_a = z
        elif grp == 3:
            glu = glu_a * _sigmoid(z)
            for q in range(lanes_per_tile):
                glu_c[jb * lanes_per_tile + q, halo:halo + rb, :] = (
                    glu[:, q * LANES:(q + 1) * LANES])
        elif grp == 0:
            u_c[:, tile_cols(jb)] = _gelu(z)
        elif grp == 1:
            vg_c[:, tile_cols(jb)] = _gelu(z)
        elif grp == 4:
            ga_c[:, tile_cols(jb)] = _sigmoid(z)
        else:
            gb_c[:, tile_cols(jb)] = _sigmoid(z)
        if idx == gating_after:
            ua = spatial_gating()

    piece = _causal_conv_packed(even_p, odd_p, n_lane_blocks - 1, rb, halo, cw_ref, cb_ref)
    conv_cols.append(piece)
    y_a = _dot_w(_order_after(ua, piece), w_a_ref[...])
    c = _layernorm(jnp.concatenate(conv_cols, axis=1), ln_b_g_ref[...], ln_b_b_ref[...])
    c = (c * _sigmoid(c)).astype(BF16)
    y_b = _dot_w(c, w_b_ref[...])

    n_rows = halo + rb
    for j in range(n_lane_blocks):
        glu_c[j, 0:halo, :] = jnp.where(first_of_seq, 0.0, glu_p[j, rb:rb + halo, :])
        even_c[j] = pltpu.bitcast(glu_c[j, 0:n_rows, :].astype(BF16), jnp.uint32)
        odd_c[j] = pltpu.bitcast(glu_c[j, 1:n_rows + 1, :].astype(BF16), jnp.uint32)

    merged = ga_p[...] * y_a + gb_p[...] * y_b
    o_ref[...] = xprev_ref[...] + _dot_w(merged.astype(BF16), w_o_ref[...])


def _mixer_kernel(*refs, blocks_per_seq, halo, width):
    n_in = 15
    in_refs, o_ref, scratch = refs[:n_in], refs[n_in], refs[n_in + 1:]
    slots = (scratch[0::2], scratch[1::2])
    t = pl.program_id(0)

    @pl.when(t == 0)
    def _():
        for ref in scratch:
            ref[...] = jnp.zeros(ref.shape, ref.dtype)

    first_of_seq = t % blocks_per_seq == 0
    for parity in range(2):
        @pl.when(t % 2 == parity)
        def _(parity=parity):
            _mixer_step(slots[parity], slots[1 - parity], first_of_seq, *in_refs, o_ref,
                        halo=halo, width=width)


def _ffn_kernel(x_ref, p_ref, g_ffn_ref, w_up_ref, fw_ref, fb_ref, w_down_ref,
                g_pg_ref, w_pg_ref, w_ple_ref, g_ple_ref, g_final_ref,
                o_ref, ubuf, act, *, steps_per_seq, halo, ffn_dim, final_norm):
    step_rows = x_ref.shape[0]
    n_up_blocks = (2 * ffn_dim) // LANES
    n_gate_blocks = ffn_dim // LANES
    lanes_per_tile = MXU_COLS // LANES

    @pl.when(pl.program_id(0) % steps_per_seq == 0)
    def _():
        ubuf[:, 0:halo, :] = jnp.zeros((n_up_blocks, halo, LANES), F32)

    sub_rows = [slice(r, r + SUB_ROWS) for r in range(0, step_rows, SUB_ROWS)]
    xs, pes = [], []
    for rows in sub_rows:
        x = x_ref[rows, :]
        xs.append(x)
        h = _rmsnorm(x, g_ffn_ref[...]).astype(BF16)
        for jb in range((2 * ffn_dim) // MXU_COLS):
            r = _dot_w(h, w_up_ref[:, jb * MXU_COLS:(jb + 1) * MXU_COLS])
            for q in range(lanes_per_tile):
                ubuf[jb * lanes_per_tile + q, halo + rows.start:halo + rows.stop, :] = (
                    r[:, q * LANES:(q + 1) * LANES])
        pes.append(_rmsnorm(_dot_w(p_ref[rows, :].astype(BF16), w_ple_ref[...]), g_ple_ref[...]))

    for rows in sub_rows:
        for j in range(n_gate_blocks):
            gate = _causal_conv(ubuf, j, rows.start, SUB_ROWS, halo, fw_ref, fb_ref)
            val = _causal_conv(ubuf, j + n_gate_blocks, rows.start, SUB_ROWS, halo, fw_ref, fb_ref)
            act[rows, j * LANES:(j + 1) * LANES] = (_gelu(gate) * val).astype(BF16)
    for j in range(n_up_blocks):
        ubuf[j, 0:halo, :] = ubuf[j, step_rows:step_rows + halo, :]

    xs = [x + _dot_w(act[rows, :], w_down_ref[...]) for x, rows in zip(xs, sub_rows)]
    for x, pe, rows in zip(xs, pes, sub_rows):
        pg = _sigmoid(_dot_w(_rmsnorm(x, g_pg_ref[...]).astype(BF16), w_pg_ref[...]))
        x = x + pe * pg
        if final_norm:
            x = _rmsnorm(x, g_final_ref[...])
        o_ref[rows, :] = x


def _resident(shape):
    zeros = (0,) * len(shape)
    return pl.BlockSpec(shape, lambda i: zeros, pipeline_mode=pl.Buffered(1))


def _row(v):
    return v.reshape(1, -1).astype(F32)


def _halo_rows(taps):
    return -(-(taps - 1) // SUBLANES) * SUBLANES


def _mixer_call(x2d, seq, g_mix, w_in, ln_v_g, ln_v_b, w_s, b_s, w_a_out,
                conv_w, conv_b, ln_b_g, ln_b_b, w_b_out, w_o):
    t, d = x2d.shape
    width = w_a_out.shape[0]
    groups, chunk, _ = w_s.shape
    assert chunk == CHUNK and groups * LANES == width and w_b_out.shape[0] == width
    assert w_in.shape[1] == 6 * width and d == width
    assert seq % MIX_ROWS == 0 and MIX_ROWS % CHUNK == 0
    halo = _halo_rows(conv_w.shape[0])
    n_blocks = t // MIX_ROWS
    tril = jnp.tril(jnp.ones((chunk, chunk), dtype=bool))
    ws = _pack_rows(jnp.where(tril[None], w_s, jnp.zeros_like(w_s)))
    bs = jnp.broadcast_to(b_s[:, :, None], (groups, chunk, LANES)).astype(F32)
    tap_bits = jax.lax.bitcast_convert_type(conv_w.astype(BF16), jnp.uint16).astype(jnp.uint32)
    conv_taps = tap_bits | (tap_bits << 16)
    operands = [
        x2d, x2d, _row(g_mix), _pack_rows(w_in), _row(ln_v_g), _row(ln_v_b), ws, bs,
        _pack_rows(w_a_out), conv_taps, _row(conv_b), _row(ln_b_g), _row(ln_b_b),
        _pack_rows(w_b_out), _pack_rows(w_o),
    ]
    cur_spec = pl.BlockSpec((MIX_ROWS, d), lambda i: (jnp.minimum(i, n_blocks - 1), 0))
    prev_spec = pl.BlockSpec((MIX_ROWS, d), lambda i: (jnp.maximum(i - 1, 0), 0))
    in_specs = [cur_spec, prev_spec] + [_resident(a.shape) for a in operands[2:]]
    body = functools.partial(_mixer_kernel, blocks_per_seq=seq // MIX_ROWS, halo=halo,
                             width=width)
    packed_glu = pltpu.VMEM((width // LANES, (halo + MIX_ROWS) // 2, LANES), jnp.uint32)
    per_slot = [
        pltpu.VMEM((width // LANES, halo + MIX_ROWS + SUBLANES, LANES), F32),
        pltpu.VMEM((MIX_ROWS, width), F32),
        pltpu.VMEM((MIX_ROWS, width), F32),
        pltpu.VMEM((MIX_ROWS, width), F32),
        pltpu.VMEM((MIX_ROWS, width), F32),
        packed_glu,
        packed_glu,
    ]
    return pl.pallas_call(
        body,
        grid=(n_blocks + 1,),
        in_specs=in_specs,
        out_specs=prev_spec,
        out_shape=jax.ShapeDtypeStruct((t, d), F32),
        scratch_shapes=[s for s in per_slot for _ in range(2)],
        compiler_params=pltpu.CompilerParams(
            dimension_semantics=("arbitrary",), vmem_limit_bytes=VMEM_LIMIT_BYTES),
        name="token_mixer",
    )(*operands)


def _ffn_call(x2d, p2d, seq, g_ffn, w_up, ffn_conv_w, ffn_conv_b, w_down,
              g_pg, w_pg, w_ple, g_ple, g_final, final_norm):
    t, d = x2d.shape
    ffn_dim = w_down.shape[0]
    assert w_up.shape[1] == 2 * ffn_dim and ffn_dim % MXU_COLS == 0
    assert seq % STEP_ROWS == 0 and STEP_ROWS % SUB_ROWS == 0
    halo = _halo_rows(ffn_conv_w.shape[0])
    operands = [
        x2d, p2d, _row(g_ffn), _pack_rows(w_up), ffn_conv_w.astype(F32), _row(ffn_conv_b),
        _pack_rows(w_down), _row(g_pg), _pack_rows(w_pg), _pack_rows(w_ple), _row(g_ple),
        _row(g_final),
    ]
    row_spec = pl.BlockSpec((STEP_ROWS, d), lambda i: (i, 0))
    p_spec = pl.BlockSpec((STEP_ROWS, p2d.shape[1]), lambda i: (i, 0))
    in_specs = [row_spec, p_spec] + [_resident(a.shape) for a in operands[2:]]
    body = functools.partial(_ffn_kernel, steps_per_seq=seq // STEP_ROWS, halo=halo,
                             ffn_dim=ffn_dim, final_norm=final_norm)
    return pl.pallas_call(
        body,
        grid=(t // STEP_ROWS,),
        in_specs=in_specs,
        out_specs=row_spec,
        out_shape=jax.ShapeDtypeStruct((t, d), F32),
        scratch_shapes=[
            pltpu.VMEM(((2 * ffn_dim) // LANES, halo + STEP_ROWS, LANES), F32),
            pltpu.VMEM((STEP_ROWS, ffn_dim), BF16),
        ],
        compiler_params=pltpu.CompilerParams(
            dimension_semantics=("arbitrary",), vmem_limit_bytes=VMEM_LIMIT_BYTES),
        name="channel_mixer",
    )(*operands)


def kernel(x, p, g_mix, w_in, ln_v_g, ln_v_b, w_s, b_s, w_a_out, conv_b_w, conv_b_b, ln_b_g, ln_b_b, w_b_out, w_o, g_ffn, w_up, ffn_conv_w, ffn_conv_b, w_down, g_pg, w_pg, w_ple, g_ple, g_final):
    bsz, seq, d = x.shape
    depth = p.shape[0]
    h = x.reshape(bsz * seq, d)
    for i in range(depth):
        h = _mixer_call(h, seq, g_mix[i], w_in[i], ln_v_g[i], ln_v_b[i], w_s[i], b_s[i],
                        w_a_out[i], conv_b_w[i], conv_b_b[i], ln_b_g[i], ln_b_b[i],
                        w_b_out[i], w_o[i])
        h = _ffn_call(h, p[i].reshape(bsz * seq, -1), seq, g_ffn[i], w_up[i], ffn_conv_w[i],
                      ffn_conv_b[i], w_down[i], g_pg[i], w_pg[i], w_ple[i], g_ple[i],
                      g_final, final_norm=(i == depth - 1))
    return h.reshape(bsz, seq, d)
```

```python
import functools
import math

import jax
import jax.numpy as jnp
from jax.experimental import pallas as pl
from jax.experimental.pallas import tpu as pltpu

F32 = jnp.float32
BF16 = jnp.bfloat16

LANES = 128
SUBLANES = 8
MXU_COLS = 256
MIX_ROWS = 256
STEP_ROWS = 512
SUB_ROWS = 256
PACK_ROWS = 512
PACK_COLS = 2048
CHUNK = 128
CONV_GROUP = 8
EPS_RMS = 1e-6
EPS_LN = 1e-5
VMEM_LIMIT_BYTES = 56 * 1024 * 1024

_GELU_C = math.sqrt(2.0 / math.pi)
_LOG2E = math.log2(math.e)


def _largest_block(size, cap, unit):
    return max(b for b in range(unit, min(size, cap) + 1, unit) if size % b == 0)


def _pack_rows(w):
    *lead, k, n = w.shape
    w2d = w.reshape(-1, n)
    assert k % 2 == 0
    rows = _largest_block(w2d.shape[0], PACK_ROWS, 2 * SUBLANES)
    cols = _largest_block(n, PACK_COLS, LANES)
    packed = pl.pallas_call(
        _pack_kernel,
        grid=(w2d.shape[0] // rows, n // cols),
        in_specs=[pl.BlockSpec((rows, cols), lambda i, j: (i, j))],
        out_specs=pl.BlockSpec((rows // 2, cols), lambda i, j: (i, j)),
        out_shape=jax.ShapeDtypeStruct((w2d.shape[0] // 2, n), jnp.uint32),
        name="pack_weight",
    )(w2d)
    return packed.reshape(*lead, k // 2, n)


def _pack_kernel(w_ref, o_ref):
    o_ref[...] = pltpu.bitcast(w_ref[...].astype(BF16), jnp.uint32)


def _unpack_rows(w_u32):
    return pltpu.bitcast(w_u32, BF16)


def _gelu(x):
    inner = x * (_GELU_C + (_GELU_C * 0.044715) * (x * x))
    return (0.5 * x) * (1.0 + jnp.tanh(inner))


def _sigmoid(x):
    return 1.0 / (1.0 + jnp.exp2(x * (-_LOG2E)))


def _rmsnorm(x, g):
    ms = jnp.mean(x * x, axis=-1, keepdims=True)
    return (x * jax.lax.rsqrt(ms + EPS_RMS)) * g


def _layernorm(x, g, b):
    mu = jnp.mean(x, axis=-1, keepdims=True)
    xc = x - mu
    var = jnp.mean(xc * xc, axis=-1, keepdims=True)
    return (xc * jax.lax.rsqrt(var + EPS_LN)) * g + b


def _dot(a, b):
    return jnp.dot(a, b, preferred_element_type=F32)


def _dot_w(a, w_u32):
    return _dot(a, _unpack_rows(w_u32))


def _causal_conv(buf, j, row0, rows, halo, w_ref, b_ref):
    taps = w_ref.shape[0]
    cols = slice(j * LANES, (j + 1) * LANES)
    acc = jnp.zeros((rows, LANES), F32) + b_ref[:, cols]
    for k in range(taps):
        lo = halo + row0 - (taps - 1) + k
        acc = acc + buf[j, lo:lo + rows, :] * w_ref[k:k + 1, cols]
    return acc


def _causal_conv_packed(even_ref, odd_ref, j, rows, halo, w_ref, b_ref):
    taps = w_ref.shape[0]
    cols = slice(j * LANES, (j + 1) * LANES)
    words = rows // 2
    acc = jnp.zeros((rows, LANES), F32) + b_ref[:, cols]
    group = None
    for k in range(taps):
        start = halo - (taps - 1) + k
        src = odd_ref if start % 2 else even_ref
        win = pltpu.bitcast(src[j, start // 2:start // 2 + words, :], BF16)
        w_k = pltpu.bitcast(jnp.broadcast_to(w_ref[k:k + 1, cols], (words, LANES)), BF16)
        group = win * w_k if group is None else group + win * w_k
        if (k + 1) % CONV_GROUP == 0 or k == taps - 1:
            acc = acc + group.astype(F32)
            group = None
    return acc


def _zeros_after(x):
    bits = jax.lax.bitcast_convert_type(x, jnp.uint32)
    return jax.lax.bitcast_convert_type((bits >> 16) >> 16, F32)


def _order_after(lhs, x):
    rows = 2 * SUBLANES
    folded = x.reshape(x.shape[0] // SUBLANES, SUBLANES, x.shape[1]).sum(axis=0)
    folded = functools.reduce(
        jnp.add, [folded[:, j:j + LANES] for j in range(0, x.shape[1], LANES)])
    zeros = _zeros_after(jnp.concatenate([folded, folded], axis=0)).astype(lhs.dtype)
    top = jnp.concatenate([lhs[0:rows, 0:LANES] + zeros, lhs[0:rows, LANES:]], axis=1)
    return jnp.concatenate([top, lhs[rows:, :]], axis=0)


def _mixer_step(cur, prev, first_of_seq, x_ref, xprev_ref, g_mix_ref, w_in_ref, ln_v_g_ref,
                ln_v_b_ref, ws_ref, bs_ref, w_a_ref, cw_ref, cb_ref, ln_b_g_ref, ln_b_b_ref,
                w_b_ref, w_o_ref, o_ref, *, halo, width):
    rb = x_ref.shape[0]
    n_lane_blocks = width // LANES
    n_chunks = rb // CHUNK
    glu_c, u_c, vg_c, ga_c, gb_c, even_c, odd_c = cur
    glu_p, u_p, vg_p, ga_p, gb_p, even_p, odd_p = prev

    n_tiles = width // MXU_COLS
    lanes_per_tile = MXU_COLS // LANES
    h = _rmsnorm(x_ref[...], g_mix_ref[...]).astype(BF16)
    v = _layernorm(vg_p[...], ln_v_g_ref[...], ln_v_b_ref[...]).astype(BF16)

    def tile_cols(jb):
        return slice(jb * MXU_COLS, (jb + 1) * MXU_COLS)

    def ztile(lhs, group, jb):
        lo = group * width + jb * MXU_COLS
        return _dot_w(lhs, w_in_ref[:, lo:lo + MXU_COLS])

    def spatial_gating():
        ua_cols = []
        for g in range(n_lane_blocks):
            cols = slice(g * LANES, (g + 1) * LANES)
            rhs = jnp.concatenate(
                [v[n * CHUNK:(n + 1) * CHUNK, cols] for n in range(n_chunks)], axis=1)
            mix = _dot(_unpack_rows(ws_ref[g]), rhs)
            mix = jnp.concatenate(
                [mix[:, n * LANES:(n + 1) * LANES] for n in range(n_chunks)], axis=0)
            bias = jnp.concatenate([bs_ref[g]] * n_chunks, axis=0)
            ua_cols.append(u_p[:, cols] * (mix + bias))
        return jnp.concatenate(ua_cols, axis=1).astype(BF16)

    order = [(grp, jb) for jb in range(n_tiles) for grp in (2, 3)]
    order += [(grp, jb) for grp in (4, 5, 0, 1) for jb in range(n_tiles)]
    conv_before = {3 * j + 3: j for j in range(n_lane_blocks - 1)}
    gating_after = 2 * n_tiles

    conv_cols = []
    lhs = h
    glu_a = None
    ua = None
    for idx, (grp, jb) in enumerate(order):
        if idx in conv_before:
            piece = _causal_conv_packed(
                even_p, odd_p, conv_before[idx], rb, halo, cw_ref, cb_ref)
            conv_cols.append(piece)
            lhs = _order_after(lhs, piece)
        z = ztile(lhs, grp, jb)
        if grp == 2:
            glu_a = z
        elif grp == 3:
            glu = glu_a * _sigmoid(z)
            for q in range(lanes_per_tile):
                glu_c[jb * lanes_per_tile + q, halo:halo + rb, :] = (
                    glu[:, q * LANES:(q + 1) * LANES])
        elif grp == 0:
            u_c[:, tile_cols(jb)] = _gelu(z)
        elif grp == 1:
            vg_c[:, tile_cols(jb)] = _gelu(z)
        elif grp == 4:
            ga_c[:, tile_cols(jb)] = _sigmoid(z)
        else:
            gb_c[:, tile_cols(jb)] = _sigmoid(z)
        if idx == gating_after:
            ua = spatial_gating()

    piece = _causal_conv_packed(even_p, odd_p, n_lane_blocks - 1, rb, halo, cw_ref, cb_ref)
    conv_cols.append(piece)
    y_a = _dot_w(_order_after(ua, piece), w_a_ref[...])
    c = _layernorm(jnp.concatenate(conv_cols, axis=1), ln_b_g_ref[...], ln_b_b_ref[...])
    c = (c * _sigmoid(c)).astype(BF16)
    y_b = _dot_w(c, w_b_ref[...])

    n_rows = halo + rb
    for j in range(n_lane_blocks):
        glu_c[j, 0:halo, :] = jnp.where(first_of_seq, 0.0, glu_p[j, rb:rb + halo, :])
        even_c[j] = pltpu.bitcast(glu_c[j, 0:n_rows, :].astype(BF16), jnp.uint32)
        odd_c[j] = pltpu.bitcast(glu_c[j, 1:n_rows + 1, :].astype(BF16), jnp.uint32)

    merged = ga_p[...] * y_a + gb_p[...] * y_b
    o_ref[...] = xprev_ref[...] + _dot_w(merged.astype(BF16), w_o_ref[...])


def _mixer_kernel(*refs, blocks_per_seq, halo, width):
    n_in = 15
    in_refs, o_ref, scratch = refs[:n_in], refs[n_in], refs[n_in + 1:]
    slots = (scratch[0::2], scratch[1::2])
    t = pl.program_id(0)

    @pl.when(t == 0)
    def _():
        for ref in scratch:
            ref[...] = jnp.zeros(ref.shape, ref.dtype)

    first_of_seq = t % blocks_per_seq == 0
    for parity in range(2):
        @pl.when(t % 2 == parity)
        def _(parity=parity):
            _mixer_step(slots[parity], slots[1 - parity], first_of_seq, *in_refs, o_ref,
                        halo=halo, width=width)


def _ffn_kernel(x_ref, p_ref, g_ffn_ref, w_up_ref, fw_ref, fb_ref, w_down_ref,
                g_pg_ref, w_pg_ref, w_ple_ref, g_ple_ref, g_final_ref,
                o_ref, ubuf, act, *, steps_per_seq, halo, ffn_dim, final_norm):
    step_rows = x_ref.shape[0]
    n_up_blocks = (2 * ffn_dim) // LANES
    n_gate_blocks = ffn_dim // LANES
    lanes_per_tile = MXU_COLS // LANES

    @pl.when(pl.program_id(0) % steps_per_seq == 0)
    def _():
        ubuf[:, 0:halo, :] = jnp.zeros((n_up_blocks, halo, LANES), F32)

    sub_rows = [slice(r, r + SUB_ROWS) for r in range(0, step_rows, SUB_ROWS)]
    xs, pes = [], []
    for rows in sub_rows:
        x = x_ref[rows, :]
        xs.append(x)
        h = _rmsnorm(x, g_ffn_ref[...]).astype(BF16)
        for jb in range((2 * ffn_dim) // MXU_COLS):
            r = _dot_w(h, w_up_ref[:, jb * MXU_COLS:(jb + 1) * MXU_COLS])
            for q in range(lanes_per_tile):
                ubuf[jb * lanes_per_tile + q, halo + rows.start:halo + rows.stop, :] = (
                    r[:, q * LANES:(q + 1) * LANES])
        pes.append(_rmsnorm(_dot_w(p_ref[rows, :].astype(BF16), w_ple_ref[...]), g_ple_ref[...]))

    for rows in sub_rows:
        for j in range(n_gate_blocks):
            gate = _causal_conv(ubuf, j, rows.start, SUB_ROWS, halo, fw_ref, fb_ref)
            val = _causal_conv(ubuf, j + n_gate_blocks, rows.start, SUB_ROWS, halo, fw_ref, fb_ref)
            act[rows, j * LANES:(j + 1) * LANES] = (_gelu(gate) * val).astype(BF16)
    for j in range(n_up_blocks):
        ubuf[j, 0:halo, :] = ubuf[j, step_rows:step_rows + halo, :]

    xs = [x + _dot_w(act[rows, :], w_down_ref[...]) for x, rows in zip(xs, sub_rows)]
    for x, pe, rows in zip(xs, pes, sub_rows):
        pg = _sigmoid(_dot_w(_rmsnorm(x, g_pg_ref[...]).astype(BF16), w_pg_ref[...]))
        x = x + pe * pg
        if final_norm:
            x = _rmsnorm(x, g_final_ref[...])
        o_ref[rows, :] = x


def _resident(shape):
    zeros = (0,) * len(shape)
    return pl.BlockSpec(shape, lambda i: zeros, pipeline_mode=pl.Buffered(1))


def _row(v):
    return v.reshape(1, -1).astype(F32)


def _halo_rows(taps):
    return -(-(taps - 1) // SUBLANES) * SUBLANES


def _mixer_call(x2d, seq, g_mix, w_in, ln_v_g, ln_v_b, w_s, b_s, w_a_out,
                conv_w, conv_b, ln_b_g, ln_b_b, w_b_out, w_o):
    t, d = x2d.shape
    width = w_a_out.shape[0]
    groups, chunk, _ = w_s.shape
    assert chunk == CHUNK and groups * LANES == width and w_b_out.shape[0] == width
    assert w_in.shape[1] == 6 * width and d == width
    assert seq % MIX_ROWS == 0 and MIX_ROWS % CHUNK == 0
    halo = _halo_rows(conv_w.shape[0])
    n_blocks = t // MIX_ROWS
    tril = jnp.tril(jnp.ones((chunk, chunk), dtype=bool))
    ws = _pack_rows(jnp.where(tril[None], w_s, jnp.zeros_like(w_s)))
    bs = jnp.broadcast_to(b_s[:, :, None], (groups, chunk, LANES)).astype(F32)
    tap_bits = jax.lax.bitcast_convert_type(conv_w.astype(BF16), jnp.uint16).astype(jnp.uint32)
    conv_taps = tap_bits | (tap_bits << 16)
    operands = [
        x2d, x2d, _row(g_mix), _pack_rows(w_in), _row(ln_v_g), _row(ln_v_b), ws, bs,
        _pack_rows(w_a_out), conv_taps, _row(conv_b), _row(ln_b_g), _row(ln_b_b),
        _pack_rows(w_b_out), _pack_rows(w_o),
    ]
    cur_spec = pl.BlockSpec((MIX_ROWS, d), lambda i: (jnp.minimum(i, n_blocks - 1), 0))
    prev_spec = pl.BlockSpec((MIX_ROWS, d), lambda i: (jnp.maximum(i - 1, 0), 0))
    in_specs = [cur_spec, prev_spec] + [_resident(a.shape) for a in operands[2:]]
    body = functools.partial(_mixer_kernel, blocks_per_seq=seq // MIX_ROWS, halo=halo,
                             width=width)
    packed_glu = pltpu.VMEM((width // LANES, (halo + MIX_ROWS) // 2, LANES), jnp.uint32)
    per_slot = [
        pltpu.VMEM((width // LANES, halo + MIX_ROWS + SUBLANES, LANES), F32),
        pltpu.VMEM((MIX_ROWS, width), F32),
        pltpu.VMEM((MIX_ROWS, width), F32),
        pltpu.VMEM((MIX_ROWS, width), F32),
        pltpu.VMEM((MIX_ROWS, width), F32),
        packed_glu,
        packed_glu,
    ]
    return pl.pallas_call(
        body,
        grid=(n_blocks + 1,),
        in_specs=in_specs,
        out_specs=prev_spec,
        out_shape=jax.ShapeDtypeStruct((t, d), F32),
        scratch_shapes=[s for s in per_slot for _ in range(2)],
        compiler_params=pltpu.CompilerParams(
            dimension_semantics=("arbitrary",), vmem_limit_bytes=VMEM_LIMIT_BYTES),
        name="token_mixer",
    )(*operands)


def _ffn_call(x2d, p2d, seq, g_ffn, w_up, ffn_conv_w, ffn_conv_b, w_down,
              g_pg, w_pg, w_ple, g_ple, g_final, final_norm):
    t, d = x2d.shape
    ffn_dim = w_down.shape[0]
    assert w_up.shape[1] == 2 * ffn_dim and ffn_dim % MXU_COLS == 0
    assert seq % STEP_ROWS == 0 and STEP_ROWS % SUB_ROWS == 0
    halo = _halo_rows(ffn_conv_w.shape[0])
    operands = [
        x2d, p2d, _row(g_ffn), _pack_rows(w_up), ffn_conv_w.astype(F32), _row(ffn_conv_b),
        _pack_rows(w_down), _row(g_pg), _pack_rows(w_pg), _pack_rows(w_ple), _row(g_ple),
        _row(g_final),
    ]
    row_spec = pl.BlockSpec((STEP_ROWS, d), lambda i: (i, 0))
    p_spec = pl.BlockSpec((STEP_ROWS, p2d.shape[1]), lambda i: (i, 0))
    in_specs = [row_spec, p_spec] + [_resident(a.shape) for a in operands[2:]]
    body = functools.partial(_ffn_kernel, steps_per_seq=seq // STEP_ROWS, halo=halo,
                             ffn_dim=ffn_dim, final_norm=final_norm)
    return pl.pallas_call(
        body,
        grid=(t // STEP_ROWS,),
        in_specs=in_specs,
        out_specs=row_spec,
        out_shape=jax.ShapeDtypeStruct((t, d), F32),
        scratch_shapes=[
            pltpu.VMEM(((2 * ffn_dim) // LANES, halo + STEP_ROWS, LANES), F32),
            pltpu.VMEM((STEP_ROWS, ffn_dim), BF16),
        ],
        compiler_params=pltpu.CompilerParams(
            dimension_semantics=("arbitrary",), vmem_limit_bytes=VMEM_LIMIT_BYTES),
        name="channel_mixer",
    )(*operands)


def kernel(x, p, g_mix, w_in, ln_v_g, ln_v_b, w_s, b_s, w_a_out, conv_b_w, conv_b_b, ln_b_g, ln_b_b, w_b_out, w_o, g_ffn, w_up, ffn_conv_w, ffn_conv_b, w_down, g_pg, w_pg, w_ple, g_ple, g_final):
    bsz, seq, d = x.shape
    depth = p.shape[0]
    h = x.reshape(bsz * seq, d)
    for i in range(depth):
        h = _mixer_call(h, seq, g_mix[i], w_in[i], ln_v_g[i], ln_v_b[i], w_s[i], b_s[i],
                        w_a_out[i], conv_b_w[i], conv_b_b[i], ln_b_g[i], ln_b_b[i],
                        w_b_out[i], w_o[i])
        h = _ffn_call(h, p[i].reshape(bsz * seq, -1), seq, g_ffn[i], w_up[i], ffn_conv_w[i],
                      ffn_conv_b[i], w_down[i], g_pg[i], w_pg[i], w_ple[i], g_ple[i],
                      g_final, final_norm=(i == depth - 1))
    return h.reshape(bsz, seq, d)
```

```python
import functools
import math

import jax
import jax.numpy as jnp
from jax.experimental import pallas as pl
from jax.experimental.pallas import tpu as pltpu

F32 = jnp.float32
BF16 = jnp.bfloat16

LANES = 128
SUBLANES = 8
MXU_COLS = 256
MIX_ROWS = 256
STEP_ROWS = 512
SUB_ROWS = 128
PACK_ROWS = 512
PACK_COLS = 2048
CHUNK = 128
CONV_GROUP = 4
EPS_RMS = 1e-6
EPS_LN = 1e-5
VMEM_LIMIT_BYTES = 56 * 1024 * 1024

_GELU_C = math.sqrt(2.0 / math.pi)
_LOG2E = math.log2(math.e)


def _largest_block(size, cap, unit):
    return max(b for b in range(unit, min(size, cap) + 1, unit) if size % b == 0)


def _pack_rows(w):
    *lead, k, n = w.shape
    w2d = w.reshape(-1, n)
    assert k % 2 == 0
    rows = _largest_block(w2d.shape[0], PACK_ROWS, 2 * SUBLANES)
    cols = _largest_block(n, PACK_COLS, LANES)
    packed = pl.pallas_call(
        _pack_kernel,
        grid=(w2d.shape[0] // rows, n // cols),
        in_specs=[pl.BlockSpec((rows, cols), lambda i, j: (i, j))],
        out_specs=pl.BlockSpec((rows // 2, cols), lambda i, j: (i, j)),
        out_shape=jax.ShapeDtypeStruct((w2d.shape[0] // 2, n), jnp.uint32),
        name="pack_weight",
    )(w2d)
    return packed.reshape(*lead, k // 2, n)


def _pack_kernel(w_ref, o_ref):
    o_ref[...] = pltpu.bitcast(w_ref[...].astype(BF16), jnp.uint32)


def _unpack_rows(w_u32):
    return pltpu.bitcast(w_u32, BF16)


def _gelu(x):
    inner = x * (_GELU_C + (_GELU_C * 0.044715) * (x * x))
    return (0.5 * x) * (1.0 + jnp.tanh(inner))


def _sigmoid(x):
    return 1.0 / (1.0 + jnp.exp2(x * (-_LOG2E)))


def _rmsnorm(x, g):
    ms = jnp.mean(x * x, axis=-1, keepdims=True)
    return (x * jax.lax.rsqrt(ms + EPS_RMS)) * g


def _layernorm(x, g, b):
    mu = jnp.mean(x, axis=-1, keepdims=True)
    xc = x - mu
    var = jnp.mean(xc * xc, axis=-1, keepdims=True)
    return (xc * jax.lax.rsqrt(var + EPS_LN)) * g + b


def _dot(a, b):
    return jnp.dot(a, b, preferred_element_type=F32)


def _dot_w(a, w_u32):
    return _dot(a, _unpack_rows(w_u32))


def _causal_conv(buf, j, row0, rows, halo, w_ref, b_ref):
    taps = w_ref.shape[0]
    cols = slice(j * LANES, (j + 1) * LANES)
    acc = jnp.zeros((rows, LANES), F32) + b_ref[:, cols]
    for k in range(taps):
        lo = halo + row0 - (taps - 1) + k
        acc = acc + buf[j, lo:lo + rows, :] * w_ref[k:k + 1, cols]
    return acc


def _causal_conv_packed(even_ref, odd_ref, j, rows, halo, w_ref, b_ref):
    taps = w_ref.shape[0]
    cols = slice(j * LANES, (j + 1) * LANES)
    words = rows // 2
    acc = jnp.zeros((rows, LANES), F32) + b_ref[:, cols]
    group = None
    for k in range(taps):
        start = halo - (taps - 1) + k
        src = odd_ref if start % 2 else even_ref
        win = pltpu.bitcast(src[j, start // 2:start // 2 + words, :], BF16)
        w_k = pltpu.bitcast(jnp.broadcast_to(w_ref[k:k + 1, cols], (words, LANES)), BF16)
        group = win * w_k if group is None else group + win * w_k
        if (k + 1) % CONV_GROUP == 0 or k == taps - 1:
            acc = acc + group.astype(F32)
            group = None
    return acc


def _zeros_after(x):
    bits = jax.lax.bitcast_convert_type(x, jnp.uint32)
    return jax.lax.bitcast_convert_type((bits >> 16) >> 16, F32)


def _order_after(lhs, x):
    rows = 2 * SUBLANES
    folded = x.reshape(x.shape[0] // SUBLANES, SUBLANES, x.shape[1]).sum(axis=0)
    folded = functools.reduce(
        jnp.add, [folded[:, j:j + LANES] for j in range(0, x.shape[1], LANES)])
    zeros = _zeros_after(jnp.concatenate([folded, folded], axis=0)).astype(lhs.dtype)
    top = jnp.concatenate([lhs[0:rows, 0:LANES] + zeros, lhs[0:rows, LANES:]], axis=1)
    return jnp.concatenate([top, lhs[rows:, :]], axis=0)


def _mixer_step(cur, prev, first_of_seq, x_ref, xprev_ref, g_mix_ref, w_in_ref, ln_v_g_ref,
                ln_v_b_ref, ws_ref, bs_ref, w_a_ref, cw_ref, cb_ref, ln_b_g_ref, ln_b_b_ref,
                w_b_ref, w_o_ref, o_ref, *, halo, width):
    rb = x_ref.shape[0]
    n_lane_blocks = width // LANES
    n_chunks = rb // CHUNK
    glu_c, u_c, vg_c, ga_c, gb_c, even_c, odd_c = cur
    glu_p, u_p, vg_p, ga_p, gb_p, even_p, odd_p = prev

    n_tiles = width // MXU_COLS
    lanes_per_tile = MXU_COLS // LANES
    h = _rmsnorm(x_ref[...], g_mix_ref[...]).astype(BF16)
    v = _layernorm(vg_p[...], ln_v_g_ref[...], ln_v_b_ref[...]).astype(BF16)

    def tile_cols(jb):
        return slice(jb * MXU_COLS, (jb + 1) * MXU_COLS)

    def ztile(lhs, group, jb):
        lo = group * width + jb * MXU_COLS
        return _dot_w(lhs, w_in_ref[:, lo:lo + MXU_COLS])

    def spatial_gating():
        ua_cols = []
        for g in range(n_lane_blocks):
            cols = slice(g * LANES, (g + 1) * LANES)
            rhs = jnp.concatenate(
                [v[n * CHUNK:(n + 1) * CHUNK, cols] for n in range(n_chunks)], axis=1)
            mix = _dot(_unpack_rows(ws_ref[g]), rhs)
            mix = jnp.concatenate(
                [mix[:, n * LANES:(n + 1) * LANES] for n in range(n_chunks)], axis=0)
            bias = jnp.concatenate([bs_ref[g]] * n_chunks, axis=0)
            ua_cols.append(u_p[:, cols] * (mix + bias))
        return jnp.concatenate(ua_cols, axis=1).astype(BF16)

    order = [(grp, jb) for jb in range(n_tiles) for grp in (2, 3)]
    order += [(grp, jb) for grp in (4, 5, 0, 1) for jb in range(n_tiles)]
    conv_before = {3 * j + 3: j for j in range(n_lane_blocks - 1)}
    gating_after = 2 * n_tiles

    conv_cols = []
    lhs = h
    glu_a = None
    ua = None
    for idx, (grp, jb) in enumerate(order):
        if idx in conv_before:
            piece = _causal_conv_packed(
                even_p, odd_p, conv_before[idx], rb, halo, cw_ref, cb_ref)
            conv_cols.append(piece)
            lhs = _order_after(lhs, piece)
        z = ztile(lhs, grp, jb)
        if grp == 2:
            glu_a = z
        elif grp == 3:
            glu = glu_a * _sigmoid(z)
            for q in range(lanes_per_tile):
                glu_c[jb * lanes_per_tile + q, halo:halo + rb, :] = (
                    glu[:, q * LANES:(q + 1) * LANES])
        elif grp == 0:
            u_c[:, tile_cols(jb)] = _gelu(z)
        elif grp == 1:
            vg_c[:, tile_cols(jb)] = _gelu(z)
        elif grp == 4:
            ga_c[:, tile_cols(jb)] = _sigmoid(z)
        else:
            gb_c[:, tile_cols(jb)] = _sigmoid(z)
        if idx == gating_after:
            ua = spatial_gating()

    piece = _causal_conv_packed(even_p, odd_p, n_lane_blocks - 1, rb, halo, cw_ref, cb_ref)
    conv_cols.append(piece)
    y_a = _dot_w(_order_after(ua, piece), w_a_ref[...])
    c = _layernorm(jnp.concatenate(conv_cols, axis=1), ln_b_g_ref[...], ln_b_b_ref[...])
    c = (c * _sigmoid(c)).astype(BF16)
    y_b = _dot_w(c, w_b_ref[...])

    n_rows = halo + rb
    for j in range(n_lane_blocks):
        glu_c[j, 0:halo, :] = jnp.where(first_of_seq, 0.0, glu_p[j, rb:rb + halo, :])
        even_c[j] = pltpu.bitcast(glu_c[j, 0:n_rows, :].astype(BF16), jnp.uint32)
        odd_c[j] = pltpu.bitcast(glu_c[j, 1:n_rows + 1, :].astype(BF16), jnp.uint32)

    merged = ga_p[...] * y_a + gb_p[...] * y_b
    o_ref[...] = xprev_ref[...] + _dot_w(merged.astype(BF16), w_o_ref[...])


def _mixer_kernel(*refs, blocks_per_seq, halo, width):
    n_in = 15
    in_refs, o_ref, scratch = refs[:n_in], refs[n_in], refs[n_in + 1:]
    slots = (scratch[0::2], scratch[1::2])
    t = pl.program_id(0)

    @pl.when(t == 0)
    def _():
        for ref in scratch:
            ref[...] = jnp.zeros(ref.shape, ref.dtype)

    first_of_seq = t % blocks_per_seq == 0
    for parity in range(2):
        @pl.when(t % 2 == parity)
        def _(parity=parity):
            _mixer_step(slots[parity], slots[1 - parity], first_of_seq, *in_refs, o_ref,
                        halo=halo, width=width)


def _ffn_kernel(x_ref, p_ref, g_ffn_ref, w_up_ref, fw_ref, fb_ref, w_down_ref,
                g_pg_ref, w_pg_ref, w_ple_ref, g_ple_ref, g_final_ref,
                o_ref, ubuf, act, *, steps_per_seq, halo, ffn_dim, final_norm):
    step_rows = x_ref.shape[0]
    n_up_blocks = (2 * ffn_dim) // LANES
    n_gate_blocks = ffn_dim // LANES
    lanes_per_tile = MXU_COLS // LANES

    @pl.when(pl.program_id(0) % steps_per_seq == 0)
    def _():
        ubuf[:, 0:halo, :] = jnp.zeros((n_up_blocks, halo, LANES), F32)

    sub_rows = [slice(r, r + SUB_ROWS) for r in range(0, step_rows, SUB_ROWS)]
    xs, pes = [], []
    for rows in sub_rows:
        x = x_ref[rows, :]
        xs.append(x)
        h = _rmsnorm(x, g_ffn_ref[...]).astype(BF16)
        for jb in range((2 * ffn_dim) // MXU_COLS):
            r = _dot_w(h, w_up_ref[:, jb * MXU_COLS:(jb + 1) * MXU_COLS])
            for q in range(lanes_per_tile):
                ubuf[jb * lanes_per_tile + q, halo + rows.start:halo + rows.stop, :] = (
                    r[:, q * LANES:(q + 1) * LANES])
        pes.append(_rmsnorm(_dot_w(p_ref[rows, :].astype(BF16), w_ple_ref[...]), g_ple_ref[...]))

    for rows in sub_rows:
        for j in range(n_gate_blocks):
            gate = _causal_conv(ubuf, j, rows.start, SUB_ROWS, halo, fw_ref, fb_ref)
            val = _causal_conv(ubuf, j + n_gate_blocks, rows.start, SUB_ROWS, halo, fw_ref, fb_ref)
            act[rows, j * LANES:(j + 1) * LANES] = (_gelu(gate) * val).astype(BF16)
    for j in range(n_up_blocks):
        ubuf[j, 0:halo, :] = ubuf[j, step_rows:step_rows + halo, :]

    xs = [x + _dot_w(act[rows, :], w_down_ref[...]) for x, rows in zip(xs, sub_rows)]
    for x, pe, rows in zip(xs, pes, sub_rows):
        pg = _sigmoid(_dot_w(_rmsnorm(x, g_pg_ref[...]).astype(BF16), w_pg_ref[...]))
        x = x + pe * pg
        if final_norm:
            x = _rmsnorm(x, g_final_ref[...])
        o_ref[rows, :] = x


def _resident(shape):
    zeros = (0,) * len(shape)
    return pl.BlockSpec(shape, lambda i: zeros, pipeline_mode=pl.Buffered(1))


def _row(v):
    return v.reshape(1, -1).astype(F32)


def _halo_rows(taps):
    return -(-(taps - 1) // SUBLANES) * SUBLANES


def _mixer_call(x2d, seq, g_mix, w_in, ln_v_g, ln_v_b, w_s, b_s, w_a_out,
                conv_w, conv_b, ln_b_g, ln_b_b, w_b_out, w_o):
    t, d = x2d.shape
    width = w_a_out.shape[0]
    groups, chunk, _ = w_s.shape
    assert chunk == CHUNK and groups * LANES == width and w_b_out.shape[0] == width
    assert w_in.shape[1] == 6 * width and d == width
    assert seq % MIX_ROWS == 0 and MIX_ROWS % CHUNK == 0
    halo = _halo_rows(conv_w.shape[0])
    n_blocks = t // MIX_ROWS
    tril = jnp.tril(jnp.ones((chunk, chunk), dtype=bool))
    ws = _pack_rows(jnp.where(tril[None], w_s, jnp.zeros_like(w_s)))
    bs = jnp.broadcast_to(b_s[:, :, None], (groups, chunk, LANES)).astype(F32)
    tap_bits = jax.lax.bitcast_convert_type(conv_w.astype(BF16), jnp.uint16).astype(jnp.uint32)
    conv_taps = tap_bits | (tap_bits << 16)
    operands = [
        x2d, x2d, _row(g_mix), _pack_rows(w_in), _row(ln_v_g), _row(ln_v_b), ws, bs,
        _pack_rows(w_a_out), conv_taps, _row(conv_b), _row(ln_b_g), _row(ln_b_b),
        _pack_rows(w_b_out), _pack_rows(w_o),
    ]
    cur_spec = pl.BlockSpec((MIX_ROWS, d), lambda i: (jnp.minimum(i, n_blocks - 1), 0))
    prev_spec = pl.BlockSpec((MIX_ROWS, d), lambda i: (jnp.maximum(i - 1, 0), 0))
    in_specs = [cur_spec, prev_spec] + [_resident(a.shape) for a in operands[2:]]
    body = functools.partial(_mixer_kernel, blocks_per_seq=seq // MIX_ROWS, halo=halo,
                             width=width)
    packed_glu = pltpu.VMEM((width // LANES, (halo + MIX_ROWS) // 2, LANES), jnp.uint32)
    per_slot = [
        pltpu.VMEM((width // LANES, halo + MIX_ROWS + SUBLANES, LANES), F32),
        pltpu.VMEM((MIX_ROWS, width), F32),
        pltpu.VMEM((MIX_ROWS, width), F32),
        pltpu.VMEM((MIX_ROWS, width), F32),
        pltpu.VMEM((MIX_ROWS, width), F32),
        packed_glu,
        packed_glu,
    ]
    return pl.pallas_call(
        body,
        grid=(n_blocks + 1,),
        in_specs=in_specs,
        out_specs=prev_spec,
        out_shape=jax.ShapeDtypeStruct((t, d), F32),
        scratch_shapes=[s for s in per_slot for _ in range(2)],
        compiler_params=pltpu.CompilerParams(
            dimension_semantics=("arbitrary",), vmem_limit_bytes=VMEM_LIMIT_BYTES),
        name="token_mixer",
    )(*operands)


def _ffn_call(x2d, p2d, seq, g_ffn, w_up, ffn_conv_w, ffn_conv_b, w_down,
              g_pg, w_pg, w_ple, g_ple, g_final, final_norm):
    t, d = x2d.shape
    ffn_dim = w_down.shape[0]
    assert w_up.shape[1] == 2 * ffn_dim and ffn_dim % MXU_COLS == 0
    assert seq % STEP_ROWS == 0 and STEP_ROWS % SUB_ROWS == 0
    halo = _halo_rows(ffn_conv_w.shape[0])
    operands = [
        x2d, p2d, _row(g_ffn), _pack_rows(w_up), ffn_conv_w.astype(F32), _row(ffn_conv_b),
        _pack_rows(w_down), _row(g_pg), _pack_rows(w_pg), _pack_rows(w_ple), _row(g_ple),
        _row(g_final),
    ]
    row_spec = pl.BlockSpec((STEP_ROWS, d), lambda i: (i, 0))
    p_spec = pl.BlockSpec((STEP_ROWS, p2d.shape[1]), lambda i: (i, 0))
    in_specs = [row_spec, p_spec] + [_resident(a.shape) for a in operands[2:]]
    body = functools.partial(_ffn_kernel, steps_per_seq=seq // STEP_ROWS, halo=halo,
                             ffn_dim=ffn_dim, final_norm=final_norm)
    return pl.pallas_call(
        body,
        grid=(t // STEP_ROWS,),
        in_specs=in_specs,
        out_specs=row_spec,
        out_shape=jax.ShapeDtypeStruct((t, d), F32),
        scratch_shapes=[
            pltpu.VMEM(((2 * ffn_dim) // LANES, halo + STEP_ROWS, LANES), F32),
            pltpu.VMEM((STEP_ROWS, ffn_dim), BF16),
        ],
        compiler_params=pltpu.CompilerParams(
            dimension_semantics=("arbitrary",), vmem_limit_bytes=VMEM_LIMIT_BYTES),
        name="channel_mixer",
    )(*operands)


def kernel(x, p, g_mix, w_in, ln_v_g, ln_v_b, w_s, b_s, w_a_out, conv_b_w, conv_b_b, ln_b_g, ln_b_b, w_b_out, w_o, g_ffn, w_up, ffn_conv_w, ffn_conv_b, w_down, g_pg, w_pg, w_ple, g_ple, g_final):
    bsz, seq, d = x.shape
    depth = p.shape[0]
    h = x.reshape(bsz * seq, d)
    for i in range(depth):
        h = _mixer_call(h, seq, g_mix[i], w_in[i], ln_v_g[i], ln_v_b[i], w_s[i], b_s[i],
                        w_a_out[i], conv_b_w[i], conv_b_b[i], ln_b_g[i], ln_b_b[i],
                        w_b_out[i], w_o[i])
        h = _ffn_call(h, p[i].reshape(bsz * seq, -1), seq, g_ffn[i], w_up[i], ffn_conv_w[i],
                      ffn_conv_b[i], w_down[i], g_pg[i], w_pg[i], w_ple[i], g_ple[i],
                      g_final, final_norm=(i == depth - 1))
    return h.reshape(bsz, seq, d)
```
